```python
import jax
import jax.numpy as jnp
from jax import lax
import numpy as np

D_MODEL = 1024
BATCH = 8
SEQ = 4096
DEPTH = 2

GRID_W = 64
CTX_LEN = 256
N_EVEN = (DEPTH + 1) // 2
N_ODD = DEPTH // 2
EPS = 1e-6
NEG_INF = -1e30
ROPE_THETA = 10000.0
BLOCK = 128

HG_WIDTH = D_MODEL // 2
HG_HEAD_DIM = 128
HG_HEADS = HG_WIDTH // HG_HEAD_DIM
HG_CHUNK = 64

SW_HEAD_DIM = 64
SW_Q_HEADS = (D_MODEL // 2) // SW_HEAD_DIM
SW_KV_HEADS = 2
WINDOW = 128

AB_CUTS = [HG_WIDTH, 2 * HG_WIDTH, 3 * HG_WIDTH, 4 * HG_WIDTH, 5 * HG_WIDTH,
           5 * HG_WIDTH + SW_Q_HEADS * SW_HEAD_DIM,
           5 * HG_WIDTH + (SW_Q_HEADS + SW_KV_HEADS) * SW_HEAD_DIM]
AB_IN = 5 * HG_WIDTH + (SW_Q_HEADS + 2 * SW_KV_HEADS) * SW_HEAD_DIM
AB_MIX = HG_WIDTH + SW_Q_HEADS * SW_HEAD_DIM

GA_HEAD_DIM = 128
GA_Q_HEADS = D_MODEL // GA_HEAD_DIM
GA_KV_HEADS = 2
GA_CUTS = [GA_Q_HEADS * GA_HEAD_DIM, (GA_Q_HEADS + GA_KV_HEADS) * GA_HEAD_DIM]
GA_IN = (GA_Q_HEADS + 2 * GA_KV_HEADS) * GA_HEAD_DIM

PEER_HEADS = 8
PEER_N_KEYS = 128
PEER_N_EXPERTS = PEER_N_KEYS * PEER_N_KEYS
PEER_KEY_DIM = 256
PEER_HALF = PEER_KEY_DIM // 2
PEER_TOPK = 16
PEER_TOKENS = 128

kernel_name = 'hybrid_dit_hgrn2_swa_axialgqa_peer'


def _rmsnorm(x, g):
    xf = x.astype(jnp.float32)
    y = xf * lax.rsqrt(jnp.mean(xf * xf, axis=-1, keepdims=True) + EPS)
    return (y * g.astype(jnp.float32)).astype(x.dtype)


def _axial_rope_tables(n_tokens, head_dim):
    n_rows = n_tokens // GRID_W
    row = jnp.repeat(jnp.arange(n_rows), GRID_W).astype(jnp.float32)
    col = jnp.tile(jnp.arange(GRID_W), n_rows).astype(jnp.float32)
    n_freq = head_dim // 4
    inv = ROPE_THETA ** (-jnp.arange(n_freq, dtype=jnp.float32) / n_freq)
    ang = jnp.concatenate([row[:, None] * inv, col[:, None] * inv], axis=-1)
    return jnp.cos(ang), jnp.sin(ang)


def _apply_rope(x, cos, sin):
    half = x.shape[-1] // 2
    xf = x.astype(jnp.float32)
    x1, x2 = xf[..., :half], xf[..., half:]
    cs, sn = cos[:, None, :], sin[:, None, :]
    return jnp.concatenate([x1 * cs - x2 * sn, x1 * sn + x2 * cs], axis=-1).astype(x.dtype)


def _to_groups(q, k, v):
    b, l, hq, d = q.shape
    hk = k.shape[2]
    qg = q.reshape(b, l, hk, hq // hk, d).transpose(0, 2, 3, 1, 4)
    return qg, k.transpose(0, 2, 1, 3), v.transpose(0, 2, 1, 3)


def _merge_groups(o):
    b, hk, g, l, d = o.shape
    return o.transpose(0, 3, 1, 2, 4).reshape(b, l, hk * g * d)


def _softmax_attend(q, k, v, mask, sink):
    scale = q.shape[-1] ** -0.5
    s = jnp.einsum('bhgtd,bhsd->bhgts', q, k).astype(jnp.float32) * scale
    if mask is not None:
        s = jnp.where(mask, s, NEG_INF)
    if sink is None:
        p = jax.nn.softmax(s, axis=-1)
    else:
        sk = sink.astype(jnp.float32)[None, :, :, None, None]
        m = jnp.maximum(jnp.max(s, axis=-1, keepdims=True), sk)
        e = jnp.exp(s - m)
        p = e / (jnp.sum(e, axis=-1, keepdims=True) + jnp.exp(sk - m))
    return jnp.einsum('bhgts,bhsd->bhgtd', p.astype(v.dtype), v)


def _window_latent(q, k, v, kc, vc, sink):
    b, hk, g, l, d = q.shape
    nb = l // BLOCK
    pad = [(0, 0), (0, 0), (BLOCK, BLOCK), (0, 0)]
    kp, vp = jnp.pad(k, pad), jnp.pad(v, pad)
    qb = jnp.moveaxis(q.reshape(b, hk, g, nb, BLOCK, d), 3, 0)
    q_off = jnp.arange(BLOCK)
    k_off = jnp.arange(3 * BLOCK)
    ctx_mask = jnp.ones((BLOCK, kc.shape[2]), dtype=bool)

    def one(args):
        qj, j = args
        start = j * BLOCK
        kj = lax.dynamic_slice_in_dim(kp, start, 3 * BLOCK, axis=2)
        vj = lax.dynamic_slice_in_dim(vp, start, 3 * BLOCK, axis=2)
        t = start + q_off
        s = start - BLOCK + k_off
        band = (s[None, :] >= 0) & (s[None, :] < l) & (jnp.abs(t[:, None] - s[None, :]) <= WINDOW)
        mask = jnp.concatenate([ctx_mask, band], axis=1)
        return _softmax_attend(qj, jnp.concatenate([kc, kj], axis=2),
                               jnp.concatenate([vc, vj], axis=2), mask, sink)

    out = lax.map(one, (qb, jnp.arange(nb)))
    return jnp.moveaxis(out, 0, 3).reshape(b, hk, g, l, d)


def _dense_latent(q, k_all, v_all):
    b, hk, g, l, d = q.shape
    nb = l // BLOCK
    qb = jnp.moveaxis(q.reshape(b, hk, g, nb, BLOCK, d), 3, 0)
    out = lax.map(lambda qj: _softmax_attend(qj, k_all, v_all, None, None), qb)
    return jnp.moveaxis(out, 0, 3).reshape(b, hk, g, l, d)


def _hgrn_decay(f_logit, lb):
    log_f = jnp.logaddexp(jnp.log(lb), jnp.log1p(-lb) + jax.nn.log_sigmoid(f_logit.astype(jnp.float32)))
    return log_f, -jnp.expm1(log_f)


def _hg_heads(t):
    b, l, _ = t.shape
    return t.reshape(b, l, HG_HEADS, HG_HEAD_DIM).transpose(0, 2, 1, 3)


def _gla_scan(q, k, v, log_f, s0):
    b, h, t, dk = q.shape
    dv = v.shape[-1]
    n = t // HG_CHUNK
    blocks = lambda z: jnp.moveaxis(z.reshape(b, h, n, HG_CHUNK, z.shape[-1]), 2, 0)
    causal = jnp.tril(jnp.ones((HG_CHUNK, HG_CHUNK), dtype=bool))[:, :, None]

    def step(state, inp):
        qc, kc, vc, gc = inp
        cum = jnp.cumsum(gc, axis=2)
        diff = cum[:, :, :, None, :] - cum[:, :, None, :, :]
        decay = jnp.where(causal, jnp.exp(jnp.where(causal, diff, 0.0)), 0.0)
        a = jnp.einsum('bhtk,bhsk,bhtsk->bhts', qc, kc, decay)
        o = jnp.einsum('bhtk,bhkv->bhtv', qc * jnp.exp(cum), state) + jnp.einsum('bhts,bhsv->bhtv', a, vc)
        end = cum[:, :, -1:, :]
        new_state = jnp.exp(end[:, :, 0, :])[..., None] * state + \
            jnp.einsum('bhsk,bhsv->bhkv', kc * jnp.exp(end - cum), vc)
        return new_state, o

    s_fin, o = lax.scan(step, s0, (blocks(q), blocks(k), blocks(v), blocks(log_f)))
    return jnp.moveaxis(o, 0, 2).reshape(b, h, t, dv), s_fin


def _hgrn_direction(q, k, v, lf, qc, kc, vc, lfc, reverse):
    if reverse:
        q, k, v, lf, qc, kc, vc, lfc = [jnp.flip(z, axis=2) for z in (q, k, v, lf, qc, kc, vc, lfc)]
    s0 = jnp.zeros(q.shape[:2] + (HG_HEAD_DIM, HG_HEAD_DIM), jnp.float32)
    oc, s_ctx = _gla_scan(qc, kc, vc, lfc, s0)
    o, _ = _gla_scan(q, k, v, lf, s_ctx)
    if reverse:
        o, oc = jnp.flip(o, axis=2), jnp.flip(oc, axis=2)
    return o, oc


def _mixer_ab(a, ac, w_in, w_out, lb, hg_norm_g, sink, rope, need_ctx):
    b, l, _ = a.shape
    bc, lc, _ = ac.shape
    qh, ff, fb, ih, gh, qs, ks, vs = jnp.split(a @ w_in, AB_CUTS, axis=-1)
    qhc, ffc, fbc, ihc, ghc, qsc, ksc, vsc = jnp.split(ac @ w_in, AB_CUTS, axis=-1)

    def hg_prep(q_, ff_, fb_, i_):
        lf_f, k_f = _hgrn_decay(ff_, lb)
        lf_b, k_b = _hgrn_decay(fb_, lb)
        return [_hg_heads(z) for z in (jax.nn.silu(q_.astype(jnp.float32)), i_.astype(jnp.float32),
                                       lf_f, k_f, lf_b, k_b)]
    q, v, lf_f, k_f, lf_b, k_b = hg_prep(qh, ff, fb, ih)
    qc, vc, lfc_f, kc_f, lfc_b, kc_b = hg_prep(qhc, ffc, fbc, ihc)
    o_f, oc_f = _hgrn_direction(q, k_f, v, lf_f, qc, kc_f, vc, lfc_f, reverse=False)
    o_b, oc_b = _hgrn_direction(q, k_b, v, lf_b, qc, kc_b, vc, lfc_b, reverse=True)

    def hg_out(o, g, bb, ll):
        o = _rmsnorm(o, hg_norm_g).transpose(0, 2, 1, 3).reshape(bb, ll, HG_WIDTH)
        return (o * jax.nn.silu(g.astype(jnp.float32))).astype(a.dtype)

    def sw_prep(q_, k_, v_, rope_t):
        bb, ll, _ = q_.shape
        q_ = q_.reshape(bb, ll, SW_Q_HEADS, SW_HEAD_DIM)
        k_ = k_.reshape(bb, ll, SW_KV_HEADS, SW_HEAD_DIM)
        v_ = v_.reshape(bb, ll, SW_KV_HEADS, SW_HEAD_DIM)
        if rope_t is not None:
            q_, k_ = _apply_rope(q_, *rope_t), _apply_rope(k_, *rope_t)
        return _to_groups(q_, k_, v_)
    sq, sk, sv = sw_prep(qs, ks, vs, rope)
    sqc, skc, svc = sw_prep(qsc, ksc, vsc, None)
    sink_g = sink.reshape(SW_KV_HEADS, SW_Q_HEADS // SW_KV_HEADS)
    sw = _merge_groups(_window_latent(sq, sk, sv, skc, svc, sink_g)).astype(a.dtype)

    y = (jnp.concatenate([hg_out(o_f + o_b, gh, b, l), sw], axis=-1) @ w_out).astype(a.dtype)
    if not need_ctx:
        return y, None
    swc = _merge_groups(_softmax_attend(sqc, skc, svc, None, sink_g)).astype(a.dtype)
    yc = (jnp.concatenate([hg_out(oc_f + oc_b, ghc, bc, lc), swc], axis=-1) @ w_out).astype(a.dtype)
    return y, yc


def _mixer_ga(a, ac, w_in, w_out, qn_g, kn_g, rope, need_ctx):
    def prep(t, rope_t):
        bb, ll, _ = t.shape
        q_, k_, v_ = jnp.split(t @ w_in, GA_CUTS, axis=-1)
        q_ = _rmsnorm(q_.reshape(bb, ll, GA_Q_HEADS, GA_HEAD_DIM), qn_g)
        k_ = _rmsnorm(k_.reshape(bb, ll, GA_KV_HEADS, GA_HEAD_DIM), kn_g)
        v_ = v_.reshape(bb, ll, GA_KV_HEADS, GA_HEAD_DIM)
        if rope_t is not None:
            q_, k_ = _apply_rope(q_, *rope_t), _apply_rope(k_, *rope_t)
        return _to_groups(q_, k_, v_)
    q, k, v = prep(a, rope)
    qc, kc, vc = prep(ac, None)
    k_all = jnp.concatenate([kc, k], axis=2)
    v_all = jnp.concatenate([vc, v], axis=2)
    y = (_merge_groups(_dense_latent(q, k_all, v_all)) @ w_out).astype(a.dtype)
    if not need_ctx:
        return y, None
    yc = (_merge_groups(_softmax_attend(qc, kc, vc, None, None)) @ w_out).astype(a.dtype)
    return y, yc


def _peer(x, w_q, sub_keys, u, v):
    t, d = x.shape
    nb = t // PEER_TOKENS

    def one(xb):
        qh = (xb @ w_q).reshape(PEER_TOKENS, PEER_HEADS, 2, PEER_HALF)
        s = jnp.einsum('thpd,hpnd->thpn', qh, sub_keys).astype(jnp.float32)
        s1, i1 = lax.top_k(s[:, :, 0], PEER_TOPK)
        s2, i2 = lax.top_k(s[:, :, 1], PEER_TOPK)
        cand = (s1[..., :, None] + s2[..., None, :]).reshape(PEER_TOKENS, PEER_HEADS, PEER_TOPK * PEER_TOPK)
        sc, ci = lax.top_k(cand, PEER_TOPK)
        e = jnp.take_along_axis(i1, ci // PEER_TOPK, axis=-1) * PEER_N_KEYS + \
            jnp.take_along_axis(i2, ci % PEER_TOPK, axis=-1)
        gate = jax.nn.softmax(sc, axis=-1)
        ue, ve = u[e], v[e]
        act = jax.nn.gelu(jnp.einsum('thkd,td->thk', ue, xb).astype(jnp.float32), approximate=False)
        return jnp.einsum('thk,thkd->td', (gate * act).astype(ve.dtype), ve)

    return lax.map(one, x.reshape(nb, PEER_TOKENS, d)).reshape(t, d)


def _normal(k, shape, scale):
    return jax.random.normal(k, shape, jnp.float32) * scale


def setup_inputs(seed: int = 0) -> dict:
    key = jax.random.key(seed)
    ks = jax.random.split(key, 24)
    d = D_MODEL
    return {
        'x': _normal(ks[0], (BATCH, SEQ, d), 1.0),
        'c': _normal(ks[1], (BATCH, d), 1.0),
        'ctx': _normal(ks[2], (BATCH, CTX_LEN, d), 1.0),
        'c_ctx': _normal(ks[3], (d,), 1.0),
        'ada_w': _normal(ks[4], (DEPTH, d, 6 * d), 0.5 * d ** -0.5),
        'ada_b': _normal(ks[5], (DEPTH, 6 * d), 0.01),
        'norm_mix_g': 1.0 + _normal(ks[6], (DEPTH, d), 0.05),
        'norm_ffn_g': 1.0 + _normal(ks[7], (DEPTH, d), 0.05),
        'ab_w_in': _normal(ks[8], (N_EVEN, d, AB_IN), d ** -0.5),
        'ab_w_out': _normal(ks[9], (N_EVEN, AB_MIX, d), AB_MIX ** -0.5),
        'hg_lb_logits': _normal(ks[10], (N_EVEN + 1, HG_WIDTH), 1.0),
        'hg_norm_g': 1.0 + _normal(ks[11], (N_EVEN, HG_HEAD_DIM), 0.05),
        'sw_sink': _normal(ks[12], (N_EVEN, SW_Q_HEADS), 0.5),
        'ga_w_in': _normal(ks[13], (N_ODD, d, GA_IN), d ** -0.5),
        'ga_w_out': _normal(ks[14], (N_ODD, GA_Q_HEADS * GA_HEAD_DIM, d), (GA_Q_HEADS * GA_HEAD_DIM) ** -0.5),
        'ga_q_norm_g': 1.0 + _normal(ks[15], (N_ODD, GA_HEAD_DIM), 0.05),
        'ga_k_norm_g': 1.0 + _normal(ks[16], (N_ODD, GA_HEAD_DIM), 0.05),
        'peer_w_q': _normal(ks[17], (DEPTH, d, PEER_HEADS * PEER_KEY_DIM), d ** -0.5),
        'peer_sub_keys': _normal(ks[18], (DEPTH, PEER_HEADS, 2, PEER_N_KEYS, PEER_HALF), PEER_HALF ** -0.5),
        'peer_u': _normal(ks[19], (DEPTH, PEER_N_EXPERTS, d), d ** -0.5),
        'peer_v': _normal(ks[20], (DEPTH, PEER_N_EXPERTS, d), 1.0),
        'final_norm_g': 1.0 + _normal(ks[21], (d,), 0.05),
    }


def reference(x, c, ctx, c_ctx, ada_w, ada_b, norm_mix_g, norm_ffn_g,
              ab_w_in, ab_w_out, hg_lb_logits, hg_norm_g, sw_sink,
              ga_w_in, ga_w_out, ga_q_norm_g, ga_k_norm_g,
              peer_w_q, peer_sub_keys, peer_u, peer_v, final_norm_g):
    b, l, d = x.shape
    lc = ctx.shape[1]
    rope_sw = _axial_rope_tables(l, SW_HEAD_DIM)
    rope_ga = _axial_rope_tables(l, GA_HEAD_DIM)
    lbs = jnp.cumsum(jax.nn.softmax(hg_lb_logits.astype(jnp.float32), axis=0), axis=0)
    h, hc = x, ctx
    for layer in range(DEPTH):
        need_ctx = layer < DEPTH - 1
        i = layer // 2
        mod = jax.nn.silu(c) @ ada_w[layer] + ada_b[layer]
        mod_c = jax.nn.silu(c_ctx) @ ada_w[layer] + ada_b[layer]
        sh1, sc1, gt1, sh2, sc2, gt2 = jnp.split(mod[:, None, :], 6, axis=-1)
        csh1, csc1, cgt1, csh2, csc2, cgt2 = jnp.split(mod_c, 6, axis=-1)
        a = _rmsnorm(h, norm_mix_g[layer]) * (1.0 + sc1) + sh1
        ac = _rmsnorm(hc, norm_mix_g[layer]) * (1.0 + csc1) + csh1
        if layer % 2 == 0:
            y, yc = _mixer_ab(a, ac, ab_w_in[i], ab_w_out[i], lbs[i], hg_norm_g[i], sw_sink[i], rope_sw, need_ctx)
        else:
            y, yc = _mixer_ga(a, ac, ga_w_in[i], ga_w_out[i], ga_q_norm_g[i], ga_k_norm_g[i], rope_ga, need_ctx)
        h = h + gt1 * y
        f = _rmsnorm(h, norm_ffn_g[layer]) * (1.0 + sc2) + sh2
        if need_ctx:
            hc = hc + cgt1 * yc
            fc = _rmsnorm(hc, norm_ffn_g[layer]) * (1.0 + csc2) + csh2
            tokens = jnp.concatenate([f.reshape(b * l, d), fc.reshape(b * lc, d)], axis=0)
            y_all = _peer(tokens, peer_w_q[layer], peer_sub_keys[layer], peer_u[layer], peer_v[layer])
            h = h + gt2 * y_all[:b * l].reshape(b, l, d)
            hc = hc + cgt2 * y_all[b * l:].reshape(b, lc, d)
        else:
            y_f = _peer(f.reshape(b * l, d), peer_w_q[layer], peer_sub_keys[layer], peer_u[layer], peer_v[layer])
            h = h + gt2 * y_f.reshape(b, l, d)
    return _rmsnorm(h, final_norm_g)
```

```python
import functools

import numpy as np
import jax
import jax.numpy as jnp
from jax import lax
from jax.experimental import pallas as pl
from jax.experimental.pallas import tpu as pltpu

F32 = jnp.float32
BF16 = jnp.bfloat16
HIGHEST = lax.Precision.HIGHEST

EPS = 1e-6
NEG_INF = -1e30
ROPE_THETA = 10000.0
GRID_W = 64

HG_HEAD_DIM = 128
HG_HEADS = 4
HG_WIDTH = HG_HEADS * HG_HEAD_DIM
HG_CHUNK = 64
SW_HEAD_DIM = 64
SW_Q_HEADS = 8
SW_KV_HEADS = 2
WINDOW = 128
ATT_BLOCK = 128
GA_HEAD_DIM = 128
GA_Q_HEADS = 8
GA_KV_HEADS = 2
PEER_HEADS = 8
PEER_N_KEYS = 128
PEER_HALF = 128
PEER_TOPK = 16
PEER_PAIRS = PEER_HEADS * PEER_TOPK

LANES = 128
SUBLANES = 8
ROW_BLOCK = 256
VMEM_LIMIT = 56 * 1024 * 1024

_NT = (((1,), (1,)), ((), ()))
_TN = (((0,), (0,)), ((), ()))


def _cparams(*sem):
    return pltpu.CompilerParams(dimension_semantics=sem, vmem_limit_bytes=VMEM_LIMIT)


def _sigmoid(x):
    return 1.0 / (1.0 + jnp.exp(-x))


def _silu(x):
    return x * _sigmoid(x)


def _ada_kernel(c_ref, w_ref, b_ref, o_ref):
    a = _silu(c_ref[...])
    o_ref[0] = jnp.dot(a, w_ref[0], precision=HIGHEST, preferred_element_type=F32) + b_ref[0]


def _ada(cc, ada_w, ada_b):
    depth, d, n = ada_w.shape
    rows = cc.shape[0]
    tn = 1536
    return pl.pallas_call(
        _ada_kernel,
        grid=(depth, n // tn),
        in_specs=[
            pl.BlockSpec((rows, d), lambda l, j: (0, 0)),
            pl.BlockSpec((1, d, tn), lambda l, j: (l, 0, j)),
            pl.BlockSpec((1, 1, tn), lambda l, j: (l, 0, j)),
        ],
        out_specs=pl.BlockSpec((1, rows, tn), lambda l, j: (l, 0, j)),
        out_shape=jax.ShapeDtypeStruct((depth, rows, n), F32),
        compiler_params=_cparams("parallel", "parallel"),
    )(cc, ada_w, ada_b.reshape(depth, 1, n))


def _nmm_kernel(h_ref, m_ref, g_ref, w_ref, o_ref, *f_ref, row0):
    x = h_ref[0]
    ms = jnp.mean(x * x, axis=-1, keepdims=True)
    y = x * lax.rsqrt(ms + EPS) * g_ref[...]
    m = m_ref[0, 0]
    a = (y * (1.0 + m[row0 + 1:row0 + 2]) + m[row0:row0 + 1]).astype(BF16)
    if f_ref:
        f_ref[0][0] = a
    o_ref[0] = jnp.dot(a, w_ref[...], preferred_element_type=F32)


def _nmm(h, mod, g, w, *, row0, ctx_blocks, want_f=False):
    b, r, d = h.shape
    n = w.shape[1]
    tm = ROW_BLOCK
    sel = (lambda i: jnp.minimum(i // ctx_blocks, 1)) if ctx_blocks else (lambda i: 1)
    out_shape = [jax.ShapeDtypeStruct((b, r, n), F32)]
    out_specs = [pl.BlockSpec((1, tm, n), lambda bi, i: (bi, i, 0))]
    if want_f:
        out_shape.append(jax.ShapeDtypeStruct((b, r, d), BF16))
        out_specs.append(pl.BlockSpec((1, tm, d), lambda bi, i: (bi, i, 0)))
    res = pl.pallas_call(
        functools.partial(_nmm_kernel, row0=row0),
        grid=(b, r // tm),
        in_specs=[
            pl.BlockSpec((1, tm, d), lambda bi, i: (bi, i, 0)),
            pl.BlockSpec((1, 1, 8, d), lambda bi, i: (bi, sel(i), 0, 0)),
            pl.BlockSpec((1, d), lambda bi, i: (0, 0)),
            pl.BlockSpec((d, n), lambda bi, i: (0, 0)),
        ],
        out_specs=out_specs,
        out_shape=out_shape,
        compiler_params=_cparams("parallel", "parallel"),
    )(h, mod, g.reshape(1, d), w)
    return res if want_f else res[0]


_HG_LEVELS = (32, 16, 8, 4, 2, 1)
_HG_STACK_ROWS = HG_CHUNK * (2 + 2 * len(_HG_LEVELS)) + SUBLANES


def _hg_tables():
    c = HG_CHUNK
    stacks, masks = [], []
    for rev in (0, 1):
        pos = np.arange(c) if rev == 0 else c - 1 - np.arange(c)
        tt, ss = pos[:, None], pos[None, :]
        rows = [(ss <= tt)]
        qrows, krows, lmask = [], [], []
        for m in _HG_LEVELS:
            same = (tt // (2 * m)) == (ss // (2 * m))
            pt, ps = tt % (2 * m), ss % (2 * m)
            qrows.append(same & (pt >= m) & (ps >= m) & (ps <= pt))
            krows.append(same & (pt < m) & (ps > pt) & (ps <= m - 1))
            lmask.append(same & (pt >= m) & (ps < m))
        lmask.append(tt == ss)
        rows += qrows + krows + [(ss > tt), np.ones((SUBLANES, c), bool)]
        stacks.append(np.concatenate(rows, axis=0).astype(np.float32))
        masks.append(np.stack(lmask).astype(np.float32))
    return np.stack(stacks), np.stack(masks)


def _hg_kernel(q_ref, f_ref, i_ref, lb_ref, w_ref, ml_ref, o_ref, st_ref, *, n_chunks):
    d = pl.program_id(0)
    t = pl.program_id(3)
    c = HG_CHUNK
    nl = len(_HG_LEVELS)

    @pl.when(t == 0)
    def _():
        st_ref[...] = jnp.zeros_like(st_ref)

    lb = lb_ref[...]
    w = w_ref[...]

    def chunk(ci, _):
        cc = jnp.where(d == 0, ci, n_chunks - 1 - ci)
        rows = pl.ds(pl.multiple_of(cc * c, c), c)
        x = f_ref[rows, :]
        k = (1.0 - lb) * _sigmoid(-x)
        lf = jnp.log(1.0 - k)
        q = _silu(q_ref[rows, :])
        v = i_ref[rows, :]
        e = jnp.exp(jnp.dot(w, lf, precision=HIGHEST, preferred_element_type=F32))
        st = st_ref[...]
        o = lax.dot_general(q * e[0:c], st, _NT, preferred_element_type=F32)
        a = ml_ref[nl] * lax.dot_general(q, k, _NT, preferred_element_type=F32)
        for l in range(nl):
            qs = q * e[c * (1 + l):c * (2 + l)]
            ks = k * e[c * (1 + nl + l):c * (2 + nl + l)]
            a = a + ml_ref[l] * lax.dot_general(qs, ks, _NT, preferred_element_type=F32)
        o_ref[rows, :] = o + jnp.dot(a, v, preferred_element_type=F32)
        kend = k * e[c * (1 + 2 * nl):c * (2 + 2 * nl)]
        tot = e[c * (2 + 2 * nl):c * (2 + 2 * nl) + 1]
        st_ref[...] = st * tot + lax.dot_general(v, kend, _TN, preferred_element_type=F32)
        return 0

    lax.fori_loop(0, n_chunks, chunk, 0)


def _hgrn(z, lb, ctx_len):
    b, s, _ = z.shape
    tb = ROW_BLOCK
    nt, nctx = s // tb, ctx_len // tb
    stack, masks = _hg_tables()

    def tblk(d, t):
        bwd = jnp.where(t < nctx, nctx - 1 - t, nt - 1 - (t - nctx))
        return jnp.where(d == 0, t, bwd)

    return pl.pallas_call(
        functools.partial(_hg_kernel, n_chunks=tb // HG_CHUNK),
        grid=(2, b, HG_HEADS, nt),
        in_specs=[
            pl.BlockSpec((None, tb, LANES), lambda d, bi, h, t: (bi, tblk(d, t), h)),
            pl.BlockSpec((None, tb, LANES), lambda d, bi, h, t: (bi, tblk(d, t), 4 + 4 * d + h)),
            pl.BlockSpec((None, tb, LANES), lambda d, bi, h, t: (bi, tblk(d, t), 12 + h)),
            pl.BlockSpec((1, LANES), lambda d, bi, h, t: (0, h)),
            pl.BlockSpec((None, _HG_STACK_ROWS, HG_CHUNK), lambda d, bi, h, t: (d, 0, 0)),
            pl.BlockSpec((None, len(_HG_LEVELS) + 1, HG_CHUNK, HG_CHUNK),
                         lambda d, bi, h, t: (d, 0, 0, 0)),
        ],
        out_specs=pl.BlockSpec((None, None, tb, LANES), lambda d, bi, h, t: (d, bi, tblk(d, t), h)),
        out_shape=jax.ShapeDtypeStruct((2, b, s, HG_WIDTH), F32),
        scratch_shapes=[pltpu.VMEM((HG_HEAD_DIM, HG_HEAD_DIM), F32)],
        compiler_params=_cparams("parallel", "parallel", "parallel", "arbitrary"),
    )(z, z, z, lb.reshape(1, HG_WIDTH), jnp.asarray(stack), jnp.asarray(masks))


def _rope_half(x, cos, sin_signed, half):
    width = x.shape[-1]
    if 2 * half == width:
        sw = pltpu.roll(x, half, 1)
    else:
        lane = lax.broadcasted_iota(jnp.int32, x.shape, 1)
        sw = jnp.where(lane % (2 * half) < half, pltpu.roll(x, width - half, 1), pltpu.roll(x, half, 1))
    return x * cos + sw * sin_signed


def _sw_kernel(q_ref, kc_ref, vc_ref, kp_ref, k0_ref, kn_ref, vp_ref, v0_ref, vn_ref,
               cq_ref, sq_ref, cp_ref, sp_ref, cn_ref, sn_ref, sink_ref, o_ref, *, ctx_len, seq):
    jb = pl.program_id(1)
    blk = ATT_BLOCK
    half = SW_HEAD_DIM // 2
    cq, sq = cq_ref[...], sq_ref[...]
    reps = q_ref.shape[-1] // LANES
    q = _rope_half(q_ref[0], jnp.concatenate([cq] * reps, axis=1),
                   jnp.concatenate([sq] * reps, axis=1), half)
    kw = jnp.concatenate([kp_ref[0], k0_ref[0], kn_ref[0]], axis=0)
    cw = jnp.concatenate([cp_ref[...], cq, cn_ref[...]], axis=0)
    sw = jnp.concatenate([sp_ref[...], sq, sn_ref[...]], axis=0)
    kw = _rope_half(kw, cw, sw, half)
    k_all = jnp.concatenate([kc_ref[0], kw], axis=0).astype(BF16)
    v_all = jnp.concatenate([vc_ref[0], vp_ref[0], v0_ref[0], vn_ref[0]], axis=0).astype(BF16)
    nk = ctx_len + 3 * blk
    col = lax.broadcasted_iota(jnp.int32, (blk, nk), 1)
    trow = jb * blk + lax.broadcasted_iota(jnp.int32, (blk, nk), 0)
    srow = (jb - 1) * blk + (col - ctx_len)
    band = ((trow >= ctx_len) & (srow >= ctx_len) & (srow < seq) & (jnp.abs(trow - srow) <= WINDOW))
    mask = (col < ctx_len) | band
    scale = SW_HEAD_DIM ** -0.5
    group = SW_Q_HEADS // SW_KV_HEADS
    outs = []
    for hq in range(SW_Q_HEADS):
        g = hq // group
        qh = q[:, hq * SW_HEAD_DIM:(hq + 1) * SW_HEAD_DIM].astype(BF16)
        kh = k_all[:, g * SW_HEAD_DIM:(g + 1) * SW_HEAD_DIM]
        vh = v_all[:, g * SW_HEAD_DIM:(g + 1) * SW_HEAD_DIM]
        s = lax.dot_general(qh, kh, _NT, preferred_element_type=F32) * scale
        s = jnp.where(mask, s, NEG_INF)
        sk = sink_ref[hq]
        m = jnp.maximum(jnp.max(s, axis=-1, keepdims=True), sk)
        e = jnp.exp(s - m)
        p = e / (jnp.sum(e, axis=-1, keepdims=True) + jnp.exp(sk - m))
        outs.append(jnp.dot(p.astype(BF16), vh, preferred_element_type=F32))
    o_ref[0] = jnp.concatenate(outs, axis=1)


def _sw_attention(z, cos, sin, sink, ctx_len):
    b, s, _ = z.shape
    blk = ATT_BLOCK
    nb = s // blk
    qw = SW_Q_HEADS * SW_HEAD_DIM
    qcol, kcol, vcol = 5 * HG_WIDTH // qw, 5 * HG_WIDTH // LANES + 4, 5 * HG_WIDTH // LANES + 5
    prev = lambda j: jnp.maximum(j - 1, 0)
    nxt = lambda j: jnp.minimum(j + 1, nb - 1)
    kv = lambda col, f: pl.BlockSpec((1, blk, LANES), lambda bi, j: (bi, f(j), col))
    tab = lambda f: pl.BlockSpec((blk, LANES), lambda bi, j: (f(j), 0))
    same = lambda j: j
    return pl.pallas_call(
        functools.partial(_sw_kernel, ctx_len=ctx_len, seq=s),
        grid=(b, nb),
        in_specs=[
            pl.BlockSpec((1, blk, qw), lambda bi, j: (bi, j, qcol)),
            pl.BlockSpec((1, ctx_len, LANES), lambda bi, j: (bi, 0, kcol)),
            pl.BlockSpec((1, ctx_len, LANES), lambda bi, j: (bi, 0, vcol)),
            kv(kcol, prev), kv(kcol, same), kv(kcol, nxt),
            kv(vcol, prev), kv(vcol, same), kv(vcol, nxt),
            tab(same), tab(same), tab(prev), tab(prev), tab(nxt), tab(nxt),
            pl.BlockSpec(memory_space=pltpu.SMEM),
        ],
        out_specs=pl.BlockSpec((1, blk, qw), lambda bi, j: (bi, j, 0)),
        out_shape=jax.ShapeDtypeStruct((b, s, qw), F32),
        compiler_params=_cparams("parallel", "parallel"),
    )(z, z, z, z, z, z, z, z, z, cos, sin, cos, sin, cos, sin, sink)


def _ab_out_kernel(of_ref, ob_ref, g_ref, sw_ref, h_ref, m_ref, ng_ref, w_ref, o_ref):
    o = of_ref[...] + ob_ref[...]
    ng = ng_ref[...]
    parts = []
    for hh in range(HG_HEADS):
        oh = o[:, hh * HG_HEAD_DIM:(hh + 1) * HG_HEAD_DIM]
        ms = jnp.mean(oh * oh, axis=-1, keepdims=True)
        parts.append(oh * lax.rsqrt(ms + EPS) * ng)
    hg = jnp.concatenate(parts, axis=1) * _silu(g_ref[0])
    mix = jnp.concatenate([hg, sw_ref[0]], axis=1).astype(BF16)
    y = jnp.dot(mix, w_ref[...], preferred_element_type=F32)
    o_ref[0] = h_ref[0] + m_ref[0, 0][2:3] * y


def _ab_out(o, z, sw, h, mod, ng, w, ctx_blocks):
    b, s, d = h.shape
    tm = ROW_BLOCK
    sel = lambda i: jnp.minimum(i // ctx_blocks, 1)
    return pl.pallas_call(
        _ab_out_kernel,
        grid=(b, s // tm),
        in_specs=[
            pl.BlockSpec((None, None, tm, HG_WIDTH), lambda bi, i: (0, bi, i, 0)),
            pl.BlockSpec((None, None, tm, HG_WIDTH), lambda bi, i: (1, bi, i, 0)),
            pl.BlockSpec((1, tm, HG_WIDTH), lambda bi, i: (bi, i, 4)),
            pl.BlockSpec((1, tm, sw.shape[-1]), lambda bi, i: (bi, i, 0)),
            pl.BlockSpec((1, tm, d), lambda bi, i: (bi, i, 0)),
            pl.BlockSpec((1, 1, 8, d), lambda bi, i: (bi, sel(i), 0, 0)),
            pl.BlockSpec((1, HG_HEAD_DIM), lambda bi, i: (0, 0)),
            pl.BlockSpec(w.shape, lambda bi, i: (0, 0)),
        ],
        out_specs=pl.BlockSpec((1, tm, d), lambda bi, i: (bi, i, 0)),
        out_shape=jax.ShapeDtypeStruct((b, s, d), F32),
        compiler_params=_cparams("parallel", "parallel"),
    )(o, o, z, sw, h, mod, ng.reshape(1, HG_HEAD_DIM), w)


def _ga_prep_kernel(z_ref, cos_ref, sin_ref, qg_ref, kg_ref, q_ref, k_ref, v_ref):
    z = z_ref[0]
    cos, sin = cos_ref[...], sin_ref[...]
    hd = GA_HEAD_DIM

    def norm_rope(x, g):
        ms = jnp.mean(x * x, axis=-1, keepdims=True)
        return _rope_half(x * lax.rsqrt(ms + EPS) * g, cos, sin, hd // 2)

    qg, kg = qg_ref[...], kg_ref[...]
    qs = [norm_rope(z[:, i * hd:(i + 1) * hd], qg) for i in range(GA_Q_HEADS)]
    q_ref[0] = jnp.concatenate(qs, axis=1).astype(BF16)
    k0 = GA_Q_HEADS * hd
    ks = [norm_rope(z[:, k0 + i * hd:k0 + (i + 1) * hd], kg) for i in range(GA_KV_HEADS)]
    k_ref[0] = jnp.concatenate(ks, axis=1).astype(BF16)
    v_ref[0] = z[:, k0 + GA_KV_HEADS * hd:].astype(BF16)


def _ga_prep(z, cos, sin, qg, kg):
    b, s, n = z.shape
    tm = ROW_BLOCK
    qw, kw = GA_Q_HEADS * GA_HEAD_DIM, GA_KV_HEADS * GA_HEAD_DIM
    row = lambda width: pl.BlockSpec((1, tm, width), lambda bi, i: (bi, i, 0))
    return pl.pallas_call(
        _ga_prep_kernel,
        grid=(b, s // tm),
        in_specs=[
            row(n),
            pl.BlockSpec((tm, GA_HEAD_DIM), lambda bi, i: (i, 0)),
            pl.BlockSpec((tm, GA_HEAD_DIM), lambda bi, i: (i, 0)),
            pl.BlockSpec((1, GA_HEAD_DIM), lambda bi, i: (0, 0)),
            pl.BlockSpec((1, GA_HEAD_DIM), lambda bi, i: (0, 0)),
        ],
        out_specs=[row(qw), row(kw), row(kw)],
        out_shape=[jax.ShapeDtypeStruct((b, s, qw), BF16),
                   jax.ShapeDtypeStruct((b, s, kw), BF16),
                   jax.ShapeDtypeStruct((b, s, kw), BF16)],
        compiler_params=_cparams("parallel", "parallel"),
    )(z, cos, sin, qg.reshape(1, -1), kg.reshape(1, -1))


_GA_TQ = 128
_GA_KC = 256


def _flash_kernel(q_ref, k_ref, v_ref, o_ref, m_ref, l_ref, acc_ref):
    hd = GA_HEAD_DIM
    group = GA_Q_HEADS // GA_KV_HEADS
    q = q_ref[0]
    qs = jnp.concatenate([q[:, i * hd:(i + 1) * hd] for i in range(group)], axis=0)
    scale = hd ** -0.5
    m_ref[...] = jnp.full_like(m_ref, -jnp.inf)
    l_ref[...] = jnp.zeros_like(l_ref)
    acc_ref[...] = jnp.zeros_like(acc_ref)

    def step(c, _):
        rows = pl.ds(pl.multiple_of(c * _GA_KC, _GA_KC), _GA_KC)
        kc = k_ref[0, rows, :]
        vc = v_ref[0, rows, :]
        s = lax.dot_general(qs, kc, _NT, preferred_element_type=F32) * scale
        m_old = m_ref[...]
        m_new = jnp.maximum(m_old, jnp.max(s, axis=-1, keepdims=True))
        alpha = jnp.exp(m_old - m_new)
        p = jnp.exp(s - m_new)
        l_ref[...] = alpha * l_ref[...] + jnp.sum(p, axis=-1, keepdims=True)
        acc_ref[...] = alpha * acc_ref[...] + jnp.dot(p.astype(BF16), vc, preferred_element_type=F32)
        m_ref[...] = m_new
        return 0

    lax.fori_loop(0, k_ref.shape[1] // _GA_KC, step, 0)
    o = acc_ref[...] / l_ref[...]
    tq = q.shape[0]
    o_ref[0] = jnp.concatenate([o[i * tq:(i + 1) * tq] for i in range(group)], axis=1).astype(o_ref.dtype)


def _ga_attention(q, k, v, ctx_len):
    b, s, qw = q.shape
    l = s - ctx_len
    tq = _GA_TQ
    group = GA_Q_HEADS // GA_KV_HEADS
    gw = group * GA_HEAD_DIM
    off = ctx_len // tq
    return pl.pallas_call(
        _flash_kernel,
        grid=(b, GA_KV_HEADS, l // tq),
        in_specs=[
            pl.BlockSpec((1, tq, gw), lambda bi, g, i: (bi, off + i, g)),
            pl.BlockSpec((1, s, GA_HEAD_DIM), lambda bi, g, i: (bi, 0, g)),
            pl.BlockSpec((1, s, GA_HEAD_DIM), lambda bi, g, i: (bi, 0, g)),
        ],
        out_specs=pl.BlockSpec((1, tq, gw), lambda bi, g, i: (bi, i, g)),
        out_shape=jax.ShapeDtypeStruct((b, l, qw), BF16),
        scratch_shapes=[pltpu.VMEM((group * tq, 1), F32), pltpu.VMEM((group * tq, 1), F32),
                        pltpu.VMEM((group * tq, GA_HEAD_DIM), F32)],
        compiler_params=_cparams("parallel", "parallel", "parallel"),
    )(q, k, v)


def _out_res_kernel(a_ref, h_ref, m_ref, w_ref, o_ref):
    y = jnp.dot(a_ref[0], w_ref[...], preferred_element_type=F32)
    o_ref[0] = h_ref[0] + m_ref[0, 0][2:3] * y


def _out_res(a, h, mod, w, ctx_len):
    b, l, n = a.shape
    d = h.shape[-1]
    tm = ROW_BLOCK
    off = ctx_len // tm
    return pl.pallas_call(
        _out_res_kernel,
        grid=(b, l // tm),
        in_specs=[
            pl.BlockSpec((1, tm, n), lambda bi, i: (bi, i, 0)),
            pl.BlockSpec((1, tm, d), lambda bi, i: (bi, off + i, 0)),
            pl.BlockSpec((1, 1, 8, d), lambda bi, i: (bi, 1, 0, 0)),
            pl.BlockSpec(w.shape, lambda bi, i: (0, 0)),
        ],
        out_specs=pl.BlockSpec((1, tm, d), lambda bi, i: (bi, i, 0)),
        out_shape=jax.ShapeDtypeStruct((b, l, d), F32),
        compiler_params=_cparams("parallel", "parallel"),
    )(a, h, mod, w)


def _topk_rows(s, payload=None):
    r = s.shape[0]
    iota = lax.broadcasted_iota(jnp.int32, s.shape, 0)
    vals, sel = [], []
    for _ in range(PEER_TOPK):
        m = jnp.max(s, axis=0, keepdims=True)
        i = jnp.min(jnp.where(s == m, iota, r), axis=0, keepdims=True)
        hit = iota == i
        vals.append(m)
        sel.append(i if payload is None else jnp.max(jnp.where(hit, payload, -1), axis=0, keepdims=True))
        s = jnp.where(hit, -jnp.inf, s)
    return jnp.concatenate(vals, axis=0), jnp.concatenate(sel, axis=0)


def _route_kernel(q_ref, keys_ref, ids_ref, gate_ref):
    q = q_ref[0]
    ids, gates = [], []
    for h in range(PEER_HEADS):
        tops = []
        for p in range(2):
            c0 = (h * 2 + p) * PEER_HALF
            st = lax.dot_general(keys_ref[h, p], q[:, c0:c0 + PEER_HALF], _NT,
                                 precision=HIGHEST, preferred_element_type=F32)
            tops.append(_topk_rows(st))
        (s1, i1), (s2, i2) = tops
        cand = jnp.concatenate([s1[a:a + 1] + s2 for a in range(PEER_TOPK)], axis=0)
        eid = jnp.concatenate([i1[a:a + 1] * PEER_N_KEYS + i2 for a in range(PEER_TOPK)], axis=0)
        sc, e = _topk_rows(cand, eid)
        ex = jnp.exp(sc - jnp.max(sc, axis=0, keepdims=True))
        gates.append(ex / jnp.sum(ex, axis=0, keepdims=True))
        ids.append(e)
    ids_ref[0] = jnp.concatenate(ids, axis=0).astype(F32).T.astype(jnp.int32)
    gate_ref[0] = jnp.concatenate(gates, axis=0).T


def _route(q, keys):
    b, r, n = q.shape
    tm = ROW_BLOCK
    return pl.pallas_call(
        _route_kernel,
        grid=(b, r // tm),
        in_specs=[
            pl.BlockSpec((1, tm, n), lambda bi, i: (bi, i, 0)),
            pl.BlockSpec(keys.shape, lambda bi, i: (0, 0, 0, 0)),
        ],
        out_specs=[pl.BlockSpec((1, tm, PEER_PAIRS), lambda bi, i: (bi, i, 0)),
                   pl.BlockSpec((1, tm, PEER_PAIRS), lambda bi, i: (bi, i, 0))],
        out_shape=[jax.ShapeDtypeStruct((b, r, PEER_PAIRS), jnp.int32),
                   jax.ShapeDtypeStruct((b, r, PEER_PAIRS), F32)],
        compiler_params=_cparams("parallel", "parallel"),
    )(q, keys)


_EX_TOK = 8
_EX_NP = _EX_TOK * PEER_PAIRS
_EX_UNROLL = 16
_ROW_TILES = 8


def _expert_kernel(ids0_ref, idsn_ref, f_ref, gate_ref, h_ref, m_ref, fg_ref, uv_ref, o_ref,
                   buf, sem, *, n_steps, final):
    step = pl.program_id(0) * pl.num_programs(1) + pl.program_id(1)
    slot = step % 2
    slot_rows = _EX_NP * _ROW_TILES

    def issue(ids_ref, to_slot):
        base = to_slot * slot_rows

        def body(it, _):
            tok = it // (PEER_PAIRS // _EX_UNROLL)
            c0 = (it % (PEER_PAIRS // _EX_UNROLL)) * _EX_UNROLL
            for u in range(_EX_UNROLL):
                e = ids_ref[tok, c0 + u]
                row = pl.multiple_of(base + (it * _EX_UNROLL + u) * _ROW_TILES, _ROW_TILES)
                pltpu.make_async_copy(uv_ref.at[e], buf.at[pl.ds(row, _ROW_TILES), :],
                                      sem.at[to_slot]).start()
            return 0

        lax.fori_loop(0, _EX_NP // _EX_UNROLL, body, 0)

    @pl.when(step == 0)
    def _():
        issue(ids0_ref, 0)

    @pl.when(step + 1 < n_steps)
    def _():
        issue(idsn_ref, 1 - slot)

    base = pl.multiple_of(slot * slot_rows, slot_rows)
    pltpu.make_async_copy(buf.at[pl.ds(0, slot_rows), :], buf.at[pl.ds(base, slot_rows), :],
                          sem.at[slot]).wait()

    f = f_ref[0]
    words = lambda r: buf[pl.ds(base + r, _EX_NP, stride=_ROW_TILES), :]
    a = jnp.zeros((_EX_TOK, _EX_NP), F32)
    for r in range(_ROW_TILES):
        u = pltpu.bitcast(lax.shift_left(words(r), 16), F32).astype(BF16)
        a = a + lax.dot_general(f[:, r * LANES:(r + 1) * LANES], u, _NT, preferred_element_type=F32)
    act = 0.5 * a * (1.0 + lax.erf(a * (2.0 ** -0.5)))
    gate = jnp.concatenate([gate_ref[0]] * _EX_TOK, axis=1)
    own = (lax.broadcasted_iota(jnp.int32, a.shape, 1) // PEER_PAIRS
           == lax.broadcasted_iota(jnp.int32, a.shape, 0))
    coef = jnp.where(own, gate * act, 0.0).astype(BF16)
    ys = []
    for r in range(_ROW_TILES):
        v = pltpu.bitcast(words(r) & jnp.int32(-65536), F32).astype(BF16)
        ys.append(jnp.dot(coef, v, preferred_element_type=F32))
    hn = h_ref[0] + m_ref[0, 0][5:6] * jnp.concatenate(ys, axis=1)
    if final:
        ms = jnp.mean(hn * hn, axis=-1, keepdims=True)
        hn = hn * lax.rsqrt(ms + EPS) * fg_ref[...]
    o_ref[0] = hn


def _experts(ids, gates, f, h, mod, uv, final_g, *, ctx_blocks, final):
    b, r, d = h.shape
    nb = r // _EX_TOK
    n_steps = b * nb
    ids_flat = ids.reshape(n_steps, _EX_TOK, PEER_PAIRS)
    per_ctx = ctx_blocks * (ROW_BLOCK // _EX_TOK)
    sel = (lambda i: jnp.minimum(i // per_ctx, 1)) if ctx_blocks else (lambda i: 1)
    blk = lambda width: pl.BlockSpec((1, _EX_TOK, width), lambda bi, i: (bi, i, 0))
    lin = lambda bi, i: bi * nb + i
    return pl.pallas_call(
        functools.partial(_expert_kernel, n_steps=n_steps, final=final),
        grid=(b, nb),
        in_specs=[
            pl.BlockSpec((None, _EX_TOK, PEER_PAIRS), lambda bi, i: (lin(bi, i), 0, 0),
                         memory_space=pltpu.SMEM),
            pl.BlockSpec((None, _EX_TOK, PEER_PAIRS),
                         lambda bi, i: (jnp.minimum(lin(bi, i) + 1, n_steps - 1), 0, 0),
                         memory_space=pltpu.SMEM),
            blk(d), blk(PEER_PAIRS), blk(d),
            pl.BlockSpec((1, 1, 8, d), lambda bi, i: (bi, sel(i), 0, 0)),
            pl.BlockSpec((1, d), lambda bi, i: (0, 0)),
            pl.BlockSpec(memory_space=pl.ANY),
        ],
        out_specs=blk(d),
        out_shape=jax.ShapeDtypeStruct((b, r, d), F32),
        scratch_shapes=[pltpu.VMEM((2 * _EX_NP * _ROW_TILES, LANES), jnp.int32),
                        pltpu.SemaphoreType.DMA((2,))],
        compiler_params=_cparams("arbitrary", "arbitrary"),
    )(ids_flat, ids_flat, f, gates, h, mod, final_g.reshape(1, d), uv)


def _pack_experts(u, v):
    ub = lax.bitcast_convert_type(u.astype(BF16), jnp.uint16).astype(jnp.uint32)
    vb = lax.bitcast_convert_type(v.astype(BF16), jnp.uint16).astype(jnp.uint32)
    words = lax.bitcast_convert_type(ub | (vb << 16), jnp.int32)
    return words.reshape(u.shape[0], _ROW_TILES, LANES)


def _rope_tables(seq, ctx_len, head_dim):
    n_rows = seq // GRID_W
    row = jnp.repeat(jnp.arange(n_rows), GRID_W).astype(F32)
    col = jnp.tile(jnp.arange(GRID_W), n_rows).astype(F32)
    n_freq = head_dim // 4
    inv = ROPE_THETA ** (-jnp.arange(n_freq, dtype=F32) / n_freq)
    ang = jnp.concatenate([row[:, None] * inv, col[:, None] * inv], axis=-1)
    cos = jnp.concatenate([jnp.cos(ang), jnp.cos(ang)], axis=-1)
    sin = jnp.concatenate([-jnp.sin(ang), jnp.sin(ang)], axis=-1)
    reps = LANES // head_dim
    cos, sin = jnp.tile(cos, (1, reps)), jnp.tile(sin, (1, reps))
    cos = jnp.concatenate([jnp.ones((ctx_len, LANES), F32), cos], axis=0)
    sin = jnp.concatenate([jnp.zeros((ctx_len, LANES), F32), sin], axis=0)
    return cos, sin


def kernel(x, c, ctx, c_ctx, ada_w, ada_b, norm_mix_g, norm_ffn_g, ab_w_in, ab_w_out, hg_lb_logits,
           hg_norm_g, sw_sink, ga_w_in, ga_w_out, ga_q_norm_g, ga_k_norm_g, peer_w_q, peer_sub_keys,
           peer_u, peer_v, final_norm_g):
    b, l, d = x.shape
    lc = ctx.shape[1]
    assert lc % ROW_BLOCK == 0 and l % ROW_BLOCK == 0 and ada_w.shape[0] == 2
    ctx_blocks = lc // ROW_BLOCK
    h = jnp.concatenate([ctx, x], axis=1)

    n_mod = -(-(b + 1) // SUBLANES) * SUBLANES
    cc = jnp.zeros((n_mod, d), F32).at[:b].set(c).at[b].set(c_ctx)
    mod_all = _ada(cc, ada_w, ada_b)

    def mod_for(layer):
        lat = mod_all[layer, :b].reshape(b, 1, 6, d)
        cx = jnp.broadcast_to(mod_all[layer, b].reshape(1, 1, 6, d), (b, 1, 6, d))
        m = jnp.concatenate([cx, lat], axis=1)
        return jnp.pad(m, ((0, 0), (0, 0), (0, 2), (0, 0)))

    lbs = jnp.cumsum(jax.nn.softmax(hg_lb_logits.astype(F32), axis=0), axis=0)

    m0 = mod_for(0)
    z = _nmm(h, m0, norm_mix_g[0], ab_w_in[0].astype(BF16), row0=0, ctx_blocks=ctx_blocks)
    o = _hgrn(z, lbs[0], lc)
    cos_sw, sin_sw = _rope_tables(l, lc, SW_HEAD_DIM)
    sw = _sw_attention(z, cos_sw, sin_sw, sw_sink[0], lc)
    h = _ab_out(o, z, sw, h, m0, hg_norm_g[0], ab_w_out[0].astype(BF16), ctx_blocks)
    q, f = _nmm(h, m0, norm_ffn_g[0], peer_w_q[0].astype(BF16), row0=3, ctx_blocks=ctx_blocks,
                want_f=True)
    ids, gates = _route(q, peer_sub_keys[0])
    h = _experts(ids, gates, f, h, m0, _pack_experts(peer_u[0], peer_v[0]), final_norm_g,
                 ctx_blocks=ctx_blocks, final=False)

    m1 = mod_for(1)
    z = _nmm(h, m1, norm_mix_g[1], ga_w_in[0].astype(BF16), row0=0, ctx_blocks=ctx_blocks)
    cos_ga, sin_ga = _rope_tables(l, lc, GA_HEAD_DIM)
    qg, kg, vg = _ga_prep(z, cos_ga, sin_ga, ga_q_norm_g[0], ga_k_norm_g[0])
    att = _ga_attention(qg, kg, vg, lc)
    hl = _out_res(att, h, m1, ga_w_out[0].astype(BF16), lc)
    q, f = _nmm(hl, m1, norm_ffn_g[1], peer_w_q[1].astype(BF16), row0=3, ctx_blocks=0, want_f=True)
    ids, gates = _route(q, peer_sub_keys[1])
    return _experts(ids, gates, f, hl, m1, _pack_experts(peer_u[1], peer_v[1]), final_norm_g,
                    ctx_blocks=0, final=True)
```

```python
import functools

import numpy as np
import jax
import jax.numpy as jnp
from jax import lax
from jax.experimental import pallas as pl
from jax.experimental.pallas import tpu as pltpu

F32 = jnp.float32
BF16 = jnp.bfloat16
HIGHEST = lax.Precision.HIGHEST

EPS = 1e-6
NEG_INF = -1e30
ROPE_THETA = 10000.0
GRID_W = 64

HG_HEAD_DIM = 128
HG_HEADS = 4
HG_WIDTH = HG_HEADS * HG_HEAD_DIM
HG_CHUNK = 64
SW_HEAD_DIM = 64
SW_Q_HEADS = 8
SW_KV_HEADS = 2
WINDOW = 128
ATT_BLOCK = 128
GA_HEAD_DIM = 128
GA_Q_HEADS = 8
GA_KV_HEADS = 2
PEER_HEADS = 8
PEER_N_KEYS = 128
PEER_HALF = 128
PEER_TOPK = 16
PEER_PAIRS = PEER_HEADS * PEER_TOPK

LANES = 128
SUBLANES = 8
ROW_BLOCK = 256
VMEM_LIMIT = 56 * 1024 * 1024

_NT = (((1,), (1,)), ((), ()))
_TN = (((0,), (0,)), ((), ()))


def _cparams(*sem):
    return pltpu.CompilerParams(dimension_semantics=sem, vmem_limit_bytes=VMEM_LIMIT)


def _sigmoid(x):
    return 1.0 / (1.0 + jnp.exp(-x))


def _silu(x):
    return x * _sigmoid(x)


def _ada_kernel(c_ref, w_ref, b_ref, o_ref):
    a = _silu(c_ref[...])
    o_ref[0] = jnp.dot(a, w_ref[0], precision=HIGHEST, preferred_element_type=F32) + b_ref[0]


def _ada(cc, ada_w, ada_b):
    depth, d, n = ada_w.shape
    rows = cc.shape[0]
    tn = 1536
    return pl.pallas_call(
        _ada_kernel,
        grid=(depth, n // tn),
        in_specs=[
            pl.BlockSpec((rows, d), lambda l, j: (0, 0)),
            pl.BlockSpec((1, d, tn), lambda l, j: (l, 0, j)),
            pl.BlockSpec((1, 1, tn), lambda l, j: (l, 0, j)),
        ],
        out_specs=pl.BlockSpec((1, rows, tn), lambda l, j: (l, 0, j)),
        out_shape=jax.ShapeDtypeStruct((depth, rows, n), F32),
        compiler_params=_cparams("parallel", "parallel"),
    )(cc, ada_w, ada_b.reshape(depth, 1, n))


def _nmm_kernel(h_ref, m_ref, g_ref, w_ref, o_ref, *f_ref, row0):
    x = h_ref[0]
    ms = jnp.mean(x * x, axis=-1, keepdims=True)
    y = x * lax.rsqrt(ms + EPS) * g_ref[...]
    m = m_ref[0, 0]
    a = (y * (1.0 + m[row0 + 1:row0 + 2]) + m[row0:row0 + 1]).astype(BF16)
    if f_ref:
        f_ref[0][0] = a
    o_ref[0] = jnp.dot(a, w_ref[...], preferred_element_type=F32)


def _nmm(h, mod, g, w, *, row0, ctx_blocks, want_f=False):
    b, r, d = h.shape
    n = w.shape[1]
    tm = ROW_BLOCK
    sel = (lambda i: jnp.minimum(i // ctx_blocks, 1)) if ctx_blocks else (lambda i: 1)
    out_shape = [jax.ShapeDtypeStruct((b, r, n), F32)]
    out_specs = [pl.BlockSpec((1, tm, n), lambda bi, i: (bi, i, 0))]
    if want_f:
        out_shape.append(jax.ShapeDtypeStruct((b, r, d), BF16))
        out_specs.append(pl.BlockSpec((1, tm, d), lambda bi, i: (bi, i, 0)))
    res = pl.pallas_call(
        functools.partial(_nmm_kernel, row0=row0),
        grid=(b, r // tm),
        in_specs=[
            pl.BlockSpec((1, tm, d), lambda bi, i: (bi, i, 0)),
            pl.BlockSpec((1, 1, 8, d), lambda bi, i: (bi, sel(i), 0, 0)),
            pl.BlockSpec((1, d), lambda bi, i: (0, 0)),
            pl.BlockSpec((d, n), lambda bi, i: (0, 0)),
        ],
        out_specs=out_specs,
        out_shape=out_shape,
        compiler_params=_cparams("parallel", "parallel"),
    )(h, mod, g.reshape(1, d), w)
    return res if want_f else res[0]


_HG_LEVELS = (32, 16, 8, 4, 2, 1)
_HG_STACK_ROWS = HG_CHUNK * (2 + 2 * len(_HG_LEVELS)) + SUBLANES


def _hg_tables():
    c = HG_CHUNK
    stacks, masks = [], []
    for rev in (0, 1):
        pos = np.arange(c) if rev == 0 else c - 1 - np.arange(c)
        tt, ss = pos[:, None], pos[None, :]
        rows = [(ss <= tt)]
        qrows, krows, lmask = [], [], []
        for m in _HG_LEVELS:
            same = (tt // (2 * m)) == (ss // (2 * m))
            pt, ps = tt % (2 * m), ss % (2 * m)
            qrows.append(same & (pt >= m) & (ps >= m) & (ps <= pt))
            krows.append(same & (pt < m) & (ps > pt) & (ps <= m - 1))
            lmask.append(same & (pt >= m) & (ps < m))
        lmask.append(tt == ss)
        rows += qrows + krows + [(ss > tt), np.ones((SUBLANES, c), bool)]
        stacks.append(np.tile(np.concatenate(rows, axis=0).astype(np.float32), (1, 3)))
        masks.append(np.stack(lmask).astype(np.float32))
    return np.stack(stacks), np.stack(masks)


def _hg_kernel(q_ref, f_ref, i_ref, lb_ref, w_ref, ml_ref, o_ref, st_ref, *, n_chunks):
    d = pl.program_id(0)
    t = pl.program_id(3)
    c = HG_CHUNK
    nl = len(_HG_LEVELS)

    @pl.when(t == 0)
    def _():
        st_ref[...] = jnp.zeros_like(st_ref)

    lb = lb_ref[...]
    w = w_ref[...]

    def chunk(ci, _):
        cc = jnp.where(d == 0, ci, n_chunks - 1 - ci)
        rows = pl.ds(pl.multiple_of(cc * c, c), c)
        x = f_ref[rows, :]
        k = (1.0 - lb) * _sigmoid(-x)
        lf = jnp.log(1.0 - k)
        q = _silu(q_ref[rows, :])
        v = i_ref[rows, :]
        hi = lf.astype(BF16)
        r1 = lf - hi.astype(F32)
        mid = r1.astype(BF16)
        lo = (r1 - mid.astype(F32)).astype(BF16)
        e = jnp.exp(jnp.dot(w, jnp.concatenate([hi, mid, lo], axis=0), preferred_element_type=F32))
        st = st_ref[...]
        o = lax.dot_general(q * e[0:c], st, _NT, preferred_element_type=F32)
        a = ml_ref[nl] * lax.dot_general(q, k, _NT, preferred_element_type=F32)
        for l in range(nl):
            qs = q * e[c * (1 + l):c * (2 + l)]
            ks = k * e[c * (1 + nl + l):c * (2 + nl + l)]
            a = a + ml_ref[l] * lax.dot_general(qs, ks, _NT, preferred_element_type=F32)
        o_ref[rows, :] = o + jnp.dot(a, v, preferred_element_type=F32)
        kend = k * e[c * (1 + 2 * nl):c * (2 + 2 * nl)]
        tot = e[c * (2 + 2 * nl):c * (2 + 2 * nl) + 1]
        st_ref[...] = st * tot + lax.dot_general(v, kend, _TN, preferred_element_type=F32)
        return 0

    lax.fori_loop(0, n_chunks, chunk, 0)


def _hgrn(z, lb, ctx_len):
    b, s, _ = z.shape
    tb = ROW_BLOCK
    nt, nctx = s // tb, ctx_len // tb
    stack, masks = _hg_tables()

    def tblk(d, t):
        bwd = jnp.where(t < nctx, nctx - 1 - t, nt - 1 - (t - nctx))
        return jnp.where(d == 0, t, bwd)

    return pl.pallas_call(
        functools.partial(_hg_kernel, n_chunks=tb // HG_CHUNK),
        grid=(2, b, HG_HEADS, nt),
        in_specs=[
            pl.BlockSpec((None, tb, LANES), lambda d, bi, h, t: (bi, tblk(d, t), h)),
            pl.BlockSpec((None, tb, LANES), lambda d, bi, h, t: (bi, tblk(d, t), 4 + 4 * d + h)),
            pl.BlockSpec((None, tb, LANES), lambda d, bi, h, t: (bi, tblk(d, t), 12 + h)),
            pl.BlockSpec((1, LANES), lambda d, bi, h, t: (0, h)),
            pl.BlockSpec((None, _HG_STACK_ROWS, 3 * HG_CHUNK), lambda d, bi, h, t: (d, 0, 0)),
            pl.BlockSpec((None, len(_HG_LEVELS) + 1, HG_CHUNK, HG_CHUNK),
                         lambda d, bi, h, t: (d, 0, 0, 0)),
        ],
        out_specs=pl.BlockSpec((None, None, tb, LANES), lambda d, bi, h, t: (d, bi, tblk(d, t), h)),
        out_shape=jax.ShapeDtypeStruct((2, b, s, HG_WIDTH), F32),
        scratch_shapes=[pltpu.VMEM((HG_HEAD_DIM, HG_HEAD_DIM), F32)],
        compiler_params=_cparams("parallel", "parallel", "parallel", "arbitrary"),
    )(z, z, z, lb.reshape(1, HG_WIDTH), jnp.asarray(stack, BF16), jnp.asarray(masks))


def _rope_half(x, cos, sin_signed, half):
    width = x.shape[-1]
    if 2 * half == width:
        sw = pltpu.roll(x, half, 1)
    else:
        lane = lax.broadcasted_iota(jnp.int32, x.shape, 1)
        sw = jnp.where(lane % (2 * half) < half, pltpu.roll(x, width - half, 1), pltpu.roll(x, half, 1))
    return x * cos + sw * sin_signed


def _sw_kernel(q_ref, kc_ref, vc_ref, kp_ref, k0_ref, kn_ref, vp_ref, v0_ref, vn_ref,
               cq_ref, sq_ref, cp_ref, sp_ref, cn_ref, sn_ref, sink_ref, o_ref, *, ctx_len, seq):
    jb = pl.program_id(1)
    blk = ATT_BLOCK
    half = SW_HEAD_DIM // 2
    cq, sq = cq_ref[...], sq_ref[...]
    reps = q_ref.shape[-1] // LANES
    q = _rope_half(q_ref[0], jnp.concatenate([cq] * reps, axis=1),
                   jnp.concatenate([sq] * reps, axis=1), half)
    kw = jnp.concatenate([kp_ref[0], k0_ref[0], kn_ref[0]], axis=0)
    cw = jnp.concatenate([cp_ref[...], cq, cn_ref[...]], axis=0)
    sw = jnp.concatenate([sp_ref[...], sq, sn_ref[...]], axis=0)
    kw = _rope_half(kw, cw, sw, half)
    k_all = jnp.concatenate([kc_ref[0], kw], axis=0).astype(BF16)
    v_all = jnp.concatenate([vc_ref[0], vp_ref[0], v0_ref[0], vn_ref[0]], axis=0).astype(BF16)
    nk = ctx_len + 3 * blk
    col = lax.broadcasted_iota(jnp.int32, (blk, nk), 1)
    trow = jb * blk + lax.broadcasted_iota(jnp.int32, (blk, nk), 0)
    srow = (jb - 1) * blk + (col - ctx_len)
    band = ((trow >= ctx_len) & (srow >= ctx_len) & (srow < seq) & (jnp.abs(trow - srow) <= WINDOW))
    mask = (col < ctx_len) | band
    scale = SW_HEAD_DIM ** -0.5
    group = SW_Q_HEADS // SW_KV_HEADS
    outs = []
    for hq in range(SW_Q_HEADS):
        g = hq // group
        qh = q[:, hq * SW_HEAD_DIM:(hq + 1) * SW_HEAD_DIM].astype(BF16)
        kh = k_all[:, g * SW_HEAD_DIM:(g + 1) * SW_HEAD_DIM]
        vh = v_all[:, g * SW_HEAD_DIM:(g + 1) * SW_HEAD_DIM]
        s = lax.dot_general(qh, kh, _NT, preferred_element_type=F32) * scale
        s = jnp.where(mask, s, NEG_INF)
        sk = sink_ref[hq]
        m = jnp.maximum(jnp.max(s, axis=-1, keepdims=True), sk)
        e = jnp.exp(s - m)
        p = e / (jnp.sum(e, axis=-1, keepdims=True) + jnp.exp(sk - m))
        outs.append(jnp.dot(p.astype(BF16), vh, preferred_element_type=F32))
    o_ref[0] = jnp.concatenate(outs, axis=1)


def _sw_attention(z, cos, sin, sink, ctx_len):
    b, s, _ = z.shape
    blk = ATT_BLOCK
    nb = s // blk
    qw = SW_Q_HEADS * SW_HEAD_DIM
    qcol, kcol, vcol = 5 * HG_WIDTH // qw, 5 * HG_WIDTH // LANES + 4, 5 * HG_WIDTH // LANES + 5
    prev = lambda j: jnp.maximum(j - 1, 0)
    nxt = lambda j: jnp.minimum(j + 1, nb - 1)
    kv = lambda col, f: pl.BlockSpec((1, blk, LANES), lambda bi, j: (bi, f(j), col))
    tab = lambda f: pl.BlockSpec((blk, LANES), lambda bi, j: (f(j), 0))
    same = lambda j: j
    return pl.pallas_call(
        functools.partial(_sw_kernel, ctx_len=ctx_len, seq=s),
        grid=(b, nb),
        in_specs=[
            pl.BlockSpec((1, blk, qw), lambda bi, j: (bi, j, qcol)),
            pl.BlockSpec((1, ctx_len, LANES), lambda bi, j: (bi, 0, kcol)),
            pl.BlockSpec((1, ctx_len, LANES), lambda bi, j: (bi, 0, vcol)),
            kv(kcol, prev), kv(kcol, same), kv(kcol, nxt),
            kv(vcol, prev), kv(vcol, same), kv(vcol, nxt),
            tab(same), tab(same), tab(prev), tab(prev), tab(nxt), tab(nxt),
            pl.BlockSpec(memory_space=pltpu.SMEM),
        ],
        out_specs=pl.BlockSpec((1, blk, qw), lambda bi, j: (bi, j, 0)),
        out_shape=jax.ShapeDtypeStruct((b, s, qw), F32),
        compiler_params=_cparams("parallel", "parallel"),
    )(z, z, z, z, z, z, z, z, z, cos, sin, cos, sin, cos, sin, sink)


def _ab_out_kernel(of_ref, ob_ref, g_ref, sw_ref, h_ref, m_ref, ng_ref, w_ref, o_ref):
    o = of_ref[...] + ob_ref[...]
    ng = ng_ref[...]
    parts = []
    for hh in range(HG_HEADS):
        oh = o[:, hh * HG_HEAD_DIM:(hh + 1) * HG_HEAD_DIM]
        ms = jnp.mean(oh * oh, axis=-1, keepdims=True)
        parts.append(oh * lax.rsqrt(ms + EPS) * ng)
    hg = jnp.concatenate(parts, axis=1) * _silu(g_ref[0])
    mix = jnp.concatenate([hg, sw_ref[0]], axis=1).astype(BF16)
    y = jnp.dot(mix, w_ref[...], preferred_element_type=F32)
    o_ref[0] = h_ref[0] + m_ref[0, 0][2:3] * y


def _ab_out(o, z, sw, h, mod, ng, w, ctx_blocks):
    b, s, d = h.shape
    tm = ROW_BLOCK
    sel = lambda i: jnp.minimum(i // ctx_blocks, 1)
    return pl.pallas_call(
        _ab_out_kernel,
        grid=(b, s // tm),
        in_specs=[
            pl.BlockSpec((None, None, tm, HG_WIDTH), lambda bi, i: (0, bi, i, 0)),
            pl.BlockSpec((None, None, tm, HG_WIDTH), lambda bi, i: (1, bi, i, 0)),
            pl.BlockSpec((1, tm, HG_WIDTH), lambda bi, i: (bi, i, 4)),
            pl.BlockSpec((1, tm, sw.shape[-1]), lambda bi, i: (bi, i, 0)),
            pl.BlockSpec((1, tm, d), lambda bi, i: (bi, i, 0)),
            pl.BlockSpec((1, 1, 8, d), lambda bi, i: (bi, sel(i), 0, 0)),
            pl.BlockSpec((1, HG_HEAD_DIM), lambda bi, i: (0, 0)),
            pl.BlockSpec(w.shape, lambda bi, i: (0, 0)),
        ],
        out_specs=pl.BlockSpec((1, tm, d), lambda bi, i: (bi, i, 0)),
        out_shape=jax.ShapeDtypeStruct((b, s, d), F32),
        compiler_params=_cparams("parallel", "parallel"),
    )(o, o, z, sw, h, mod, ng.reshape(1, HG_HEAD_DIM), w)


def _ga_prep_kernel(z_ref, cos_ref, sin_ref, qg_ref, kg_ref, q_ref, k_ref, v_ref):
    z = z_ref[0]
    cos, sin = cos_ref[...], sin_ref[...]
    hd = GA_HEAD_DIM

    def norm_rope(x, g):
        ms = jnp.mean(x * x, axis=-1, keepdims=True)
        return _rope_half(x * lax.rsqrt(ms + EPS) * g, cos, sin, hd // 2)

    qg, kg = qg_ref[...], kg_ref[...]
    qs = [norm_rope(z[:, i * hd:(i + 1) * hd], qg) for i in range(GA_Q_HEADS)]
    q_ref[0] = jnp.concatenate(qs, axis=1).astype(BF16)
    k0 = GA_Q_HEADS * hd
    ks = [norm_rope(z[:, k0 + i * hd:k0 + (i + 1) * hd], kg) for i in range(GA_KV_HEADS)]
    k_ref[0] = jnp.concatenate(ks, axis=1).astype(BF16)
    v_ref[0] = z[:, k0 + GA_KV_HEADS * hd:].astype(BF16)


def _ga_prep(z, cos, sin, qg, kg):
    b, s, n = z.shape
    tm = ROW_BLOCK
    qw, kw = GA_Q_HEADS * GA_HEAD_DIM, GA_KV_HEADS * GA_HEAD_DIM
    row = lambda width: pl.BlockSpec((1, tm, width), lambda bi, i: (bi, i, 0))
    return pl.pallas_call(
        _ga_prep_kernel,
        grid=(b, s // tm),
        in_specs=[
            row(n),
            pl.BlockSpec((tm, GA_HEAD_DIM), lambda bi, i: (i, 0)),
            pl.BlockSpec((tm, GA_HEAD_DIM), lambda bi, i: (i, 0)),
            pl.BlockSpec((1, GA_HEAD_DIM), lambda bi, i: (0, 0)),
            pl.BlockSpec((1, GA_HEAD_DIM), lambda bi, i: (0, 0)),
        ],
        out_specs=[row(qw), row(kw), row(kw)],
        out_shape=[jax.ShapeDtypeStruct((b, s, qw), BF16),
                   jax.ShapeDtypeStruct((b, s, kw), BF16),
                   jax.ShapeDtypeStruct((b, s, kw), BF16)],
        compiler_params=_cparams("parallel", "parallel"),
    )(z, cos, sin, qg.reshape(1, -1), kg.reshape(1, -1))


_GA_TQ = 128
_GA_KC = 256


def _flash_kernel(q_ref, k_ref, v_ref, o_ref, s_ref):
    hd = GA_HEAD_DIM
    group = GA_Q_HEADS // GA_KV_HEADS
    q = q_ref[0]
    qs = jnp.concatenate([q[:, i * hd:(i + 1) * hd] for i in range(group)], axis=0)
    scale = hd ** -0.5
    n_chunks = k_ref.shape[1] // _GA_KC
    lane_tiles = _GA_KC // LANES
    mrun = None
    for c in range(n_chunks):
        s = lax.dot_general(qs, k_ref[0, c * _GA_KC:(c + 1) * _GA_KC, :], _NT,
                            preferred_element_type=F32) * scale
        s_ref[:, c * _GA_KC:(c + 1) * _GA_KC] = s
        for j in range(lane_tiles):
            sj = s[:, j * LANES:(j + 1) * LANES]
            mrun = sj if mrun is None else jnp.maximum(mrun, sj)
    m = jnp.broadcast_to(jnp.max(mrun, axis=-1, keepdims=True), mrun.shape)
    lrun = jnp.zeros_like(mrun)
    acc = jnp.zeros((qs.shape[0], hd), F32)
    for c in range(n_chunks):
        ps = [jnp.exp(s_ref[:, c * _GA_KC + j * LANES:c * _GA_KC + (j + 1) * LANES] - m)
              for j in range(lane_tiles)]
        for pj in ps:
            lrun = lrun + pj
        p = jnp.concatenate(ps, axis=1).astype(BF16)
        acc = acc + jnp.dot(p, v_ref[0, c * _GA_KC:(c + 1) * _GA_KC, :], preferred_element_type=F32)
    o = acc / jnp.sum(lrun, axis=-1, keepdims=True)
    tq = q.shape[0]
    o_ref[0] = jnp.concatenate([o[i * tq:(i + 1) * tq] for i in range(group)], axis=1).astype(o_ref.dtype)


def _ga_attention(q, k, v, ctx_len):
    b, s, qw = q.shape
    l = s - ctx_len
    tq = _GA_TQ
    group = GA_Q_HEADS // GA_KV_HEADS
    gw = group * GA_HEAD_DIM
    off = ctx_len // tq
    return pl.pallas_call(
        _flash_kernel,
        grid=(b, GA_KV_HEADS, l // tq),
        in_specs=[
            pl.BlockSpec((1, tq, gw), lambda bi, g, i: (bi, off + i, g)),
            pl.BlockSpec((1, s, GA_HEAD_DIM), lambda bi, g, i: (bi, 0, g)),
            pl.BlockSpec((1, s, GA_HEAD_DIM), lambda bi, g, i: (bi, 0, g)),
        ],
        out_specs=pl.BlockSpec((1, tq, gw), lambda bi, g, i: (bi, i, g)),
        out_shape=jax.ShapeDtypeStruct((b, l, qw), BF16),
        scratch_shapes=[pltpu.VMEM((group * tq, s), F32)],
        compiler_params=_cparams("parallel", "parallel", "parallel"),
    )(q, k, v)


def _out_res_kernel(a_ref, h_ref, m_ref, w_ref, o_ref):
    y = jnp.dot(a_ref[0], w_ref[...], preferred_element_type=F32)
    o_ref[0] = h_ref[0] + m_ref[0, 0][2:3] * y


def _out_res(a, h, mod, w, ctx_len):
    b, l, n = a.shape
    d = h.shape[-1]
    tm = ROW_BLOCK
    off = ctx_len // tm
    return pl.pallas_call(
        _out_res_kernel,
        grid=(b, l // tm),
        in_specs=[
            pl.BlockSpec((1, tm, n), lambda bi, i: (bi, i, 0)),
            pl.BlockSpec((1, tm, d), lambda bi, i: (bi, off + i, 0)),
            pl.BlockSpec((1, 1, 8, d), lambda bi, i: (bi, 1, 0, 0)),
            pl.BlockSpec(w.shape, lambda bi, i: (0, 0)),
        ],
        out_specs=pl.BlockSpec((1, tm, d), lambda bi, i: (bi, i, 0)),
        out_shape=jax.ShapeDtypeStruct((b, l, d), F32),
        compiler_params=_cparams("parallel", "parallel"),
    )(a, h, mod, w)


_NO_RANK = 1 << 20


def _topk_rows(s, rank=None, payload=None):
    if rank is None:
        rank = lax.broadcasted_iota(jnp.int32, s.shape, 0)
    vals, sel = [], []
    for _ in range(PEER_TOPK):
        m = jnp.max(s, axis=0, keepdims=True)
        i = jnp.min(jnp.where(s == m, rank, _NO_RANK), axis=0, keepdims=True)
        hit = rank == i
        vals.append(m)
        sel.append(i if payload is None else jnp.max(jnp.where(hit, payload, -1), axis=0, keepdims=True))
        s = jnp.where(hit, -jnp.inf, s)
    return jnp.concatenate(vals, axis=0), jnp.concatenate(sel, axis=0)


def _cand_slabs():
    k = PEER_TOPK
    slabs = [('b', 0, 0), ('b', 0, 8)] + [('b', a, 0) for a in range(1, 8)] + [('a', 8, 0)]
    rank = []
    for kind, a, b in slabs:
        for j in range(SUBLANES):
            aa, bb = (a, b + j) if kind == 'b' else (a + j, b)
            rank.append(aa * k + bb if (aa + 1) * (bb + 1) <= k else _NO_RANK)
    return slabs, np.asarray(rank, np.int32)


def _route_kernel(q_ref, keys_ref, rank_ref, ids_ref, gate_ref):
    q = q_ref[0]
    slabs, _ = _cand_slabs()
    rank = jnp.concatenate([rank_ref[...]] * (q.shape[0] // LANES), axis=1)
    ids, gates = [], []
    for h in range(PEER_HEADS):
        tops = []
        for p in range(2):
            c0 = (h * 2 + p) * PEER_HALF
            st = lax.dot_general(keys_ref[h, p], q[:, c0:c0 + PEER_HALF], _NT,
                                 precision=HIGHEST, preferred_element_type=F32)
            tops.append(_topk_rows(st))
        (s1, i1), (s2, i2) = tops
        pick = lambda x, y, kind, a, b: ((x[a:a + 1], y[b:b + SUBLANES]) if kind == 'b'
                                         else (x[a:a + SUBLANES], y[b:b + 1]))
        cand = jnp.concatenate([sum(pick(s1, s2, *sl)) for sl in slabs], axis=0)
        cand = jnp.where(rank < _NO_RANK, cand, -jnp.inf)
        eid = jnp.concatenate([(lambda x, y: x * PEER_N_KEYS + y)(*pick(i1, i2, *sl)) for sl in slabs],
                              axis=0)
        sc, e = _topk_rows(cand, rank, eid)
        ex = jnp.exp(sc - jnp.max(sc, axis=0, keepdims=True))
        gates.append(ex / jnp.sum(ex, axis=0, keepdims=True))
        ids.append(e)
    ids_ref[0] = jnp.concatenate(ids, axis=0).astype(F32).T.astype(jnp.int32)
    gate_ref[0] = jnp.concatenate(gates, axis=0).T


def _route(q, keys):
    b, r, n = q.shape
    tm = ROW_BLOCK
    rank = np.tile(_cand_slabs()[1][:, None], (1, LANES))
    return pl.pallas_call(
        _route_kernel,
        grid=(b, r // tm),
        in_specs=[
            pl.BlockSpec((1, tm, n), lambda bi, i: (bi, i, 0)),
            pl.BlockSpec(keys.shape, lambda bi, i: (0, 0, 0, 0)),
            pl.BlockSpec(rank.shape, lambda bi, i: (0, 0)),
        ],
        out_specs=[pl.BlockSpec((1, tm, PEER_PAIRS), lambda bi, i: (bi, i, 0)),
                   pl.BlockSpec((1, tm, PEER_PAIRS), lambda bi, i: (bi, i, 0))],
        out_shape=[jax.ShapeDtypeStruct((b, r, PEER_PAIRS), jnp.int32),
                   jax.ShapeDtypeStruct((b, r, PEER_PAIRS), F32)],
        compiler_params=_cparams("parallel", "parallel"),
    )(q, keys, jnp.asarray(rank))


_EX_TOK = 8
_EX_NP = _EX_TOK * PEER_PAIRS
_ROW_TILES = 8
_ROW_PITCH = 12
_EX_GROUPS = 2 * _ROW_TILES


def _expert_kernel(ids_ref, idsn_ref, f_ref, gate_ref, h_ref, m_ref, fg_ref, ex_ref, uv_ref, o_ref,
                   buf_a, buf_b, sem, *, n_steps, final):
    step = pl.program_id(0) * pl.num_programs(1) + pl.program_id(1)
    per_group = _EX_NP // _EX_GROUPS

    def start_copy(e, dst, p, sem_i, priority):
        pltpu.make_async_copy(uv_ref.at[e], dst.at[pl.ds(p * _ROW_PITCH, _ROW_TILES), :],
                              sem.at[sem_i]).start(priority=priority)

    def wait_all(dst, sem_i):
        rows = pl.ds(0, _EX_NP * _ROW_TILES)
        pltpu.make_async_copy(dst.at[rows, :], dst.at[rows, :], sem.at[sem_i]).wait()

    def half(cur, half_i, issue_ids, issue_off, other, other_sem):
        rows = slice(half_i * _EX_TOK, (half_i + 1) * _EX_TOK)
        f = f_ref[0, rows, :]

        def issue(g):
            for u in range(per_group):
                p = g * per_group + u
                start_copy(issue_ids[0, issue_off + p], other, p, other_sem, u % 2)

        words = lambda r: pltpu.bitcast(cur[pl.ds(r, _EX_NP, stride=_ROW_PITCH), :], BF16)
        a = jnp.zeros((_EX_TOK, _EX_NP), F32)
        for r in range(_ROW_TILES):
            issue(r)
            u = pltpu.bitcast(lax.shift_left(cur[pl.ds(r, _EX_NP, stride=_ROW_PITCH), :], 16), F32)
            a = a + lax.dot_general(f[:, r * LANES:(r + 1) * LANES], u.astype(BF16), _NT,
                                    preferred_element_type=F32)
        act = 0.5 * a * (1.0 + lax.erf(a * (2.0 ** -0.5)))
        gate = jnp.concatenate([gate_ref[0, rows, :]] * _EX_TOK, axis=1)
        own = (lax.broadcasted_iota(jnp.int32, a.shape, 1) // PEER_PAIRS
               == lax.broadcasted_iota(jnp.int32, a.shape, 0))
        ga = jnp.where(own, gate * act, 0.0).astype(BF16)
        coef = jnp.concatenate(
            [jnp.dot(ga[:, t * PEER_PAIRS:(t + 1) * PEER_PAIRS], ex_ref[...], preferred_element_type=F32)
             for t in range(_EX_TOK)], axis=1).astype(BF16)
        ys = []
        for r in range(_ROW_TILES):
            issue(_ROW_TILES + r)
            ys.append(jnp.dot(coef, words(r), preferred_element_type=F32))
        hn = h_ref[0, rows, :] + m_ref[0, 0][5:6] * jnp.concatenate(ys, axis=1)
        if final:
            ms = jnp.mean(hn * hn, axis=-1, keepdims=True)
            hn = hn * lax.rsqrt(ms + EPS) * fg_ref[...]
        o_ref[0, rows, :] = hn

    @pl.when(step == 0)
    def _():
        def body(p, _):
            start_copy(ids_ref[0, p], buf_a, p, 0, 0)
            return 0
        lax.fori_loop(0, _EX_NP, body, 0)

    wait_all(buf_a, 0)
    half(buf_a, 0, ids_ref, _EX_NP, buf_b, 1)
    wait_all(buf_b, 1)
    half(buf_b, 1, idsn_ref, 0, buf_a, 0)

    @pl.when(step == n_steps - 1)
    def _():
        wait_all(buf_a, 0)


def _experts(ids, gates, f, h, mod, uv, final_g, *, ctx_blocks, final):
    b, r, d = h.shape
    tok = 2 * _EX_TOK
    nb = r // tok
    n_steps = b * nb
    ids_flat = ids.reshape(n_steps, 1, tok * PEER_PAIRS)
    per_ctx = ctx_blocks * (ROW_BLOCK // tok)
    sel = (lambda i: jnp.minimum(i // per_ctx, 1)) if ctx_blocks else (lambda i: 1)
    blk = lambda width: pl.BlockSpec((1, tok, width), lambda bi, i: (bi, i, 0))
    lin = lambda bi, i: bi * nb + i
    expand = np.zeros((PEER_PAIRS, 2 * PEER_PAIRS), np.float32)
    expand[np.arange(PEER_PAIRS), 2 * np.arange(PEER_PAIRS) + 1] = 1.0
    buf = pltpu.VMEM((_EX_NP * _ROW_PITCH, LANES), jnp.int32)
    return pl.pallas_call(
        functools.partial(_expert_kernel, n_steps=n_steps, final=final),
        grid=(b, nb),
        in_specs=[
            pl.BlockSpec((None, 1, tok * PEER_PAIRS), lambda bi, i: (lin(bi, i), 0, 0),
                         memory_space=pltpu.SMEM),
            pl.BlockSpec((None, 1, tok * PEER_PAIRS),
                         lambda bi, i: (jnp.minimum(lin(bi, i) + 1, n_steps - 1), 0, 0),
                         memory_space=pltpu.SMEM),
            blk(d), blk(PEER_PAIRS), blk(d),
            pl.BlockSpec((1, 1, 8, d), lambda bi, i: (bi, sel(i), 0, 0)),
            pl.BlockSpec((1, d), lambda bi, i: (0, 0)),
            pl.BlockSpec(expand.shape, lambda bi, i: (0, 0)),
            pl.BlockSpec(memory_space=pl.ANY),
        ],
        out_specs=blk(d),
        out_shape=jax.ShapeDtypeStruct((b, r, d), F32),
        scratch_shapes=[buf, buf, pltpu.SemaphoreType.DMA((2,))],
        compiler_params=_cparams("arbitrary", "arbitrary"),
    )(ids_flat, ids_flat, f, gates, h, mod, final_g.reshape(1, d), jnp.asarray(expand, BF16), uv)


def _pack_experts(u, v):
    ub = lax.bitcast_convert_type(u.astype(BF16), jnp.uint16).astype(jnp.uint32)
    vb = lax.bitcast_convert_type(v.astype(BF16), jnp.uint16).astype(jnp.uint32)
    words = lax.bitcast_convert_type(ub | (vb << 16), jnp.int32)
    return words.reshape(u.shape[0], _ROW_TILES, LANES)


def _rope_tables(seq, ctx_len, head_dim):
    n_rows = seq // GRID_W
    row = jnp.repeat(jnp.arange(n_rows), GRID_W).astype(F32)
    col = jnp.tile(jnp.arange(GRID_W), n_rows).astype(F32)
    n_freq = head_dim // 4
    inv = ROPE_THETA ** (-jnp.arange(n_freq, dtype=F32) / n_freq)
    ang = jnp.concatenate([row[:, None] * inv, col[:, None] * inv], axis=-1)
    cos = jnp.concatenate([jnp.cos(ang), jnp.cos(ang)], axis=-1)
    sin = jnp.concatenate([-jnp.sin(ang), jnp.sin(ang)], axis=-1)
    reps = LANES // head_dim
    cos, sin = jnp.tile(cos, (1, reps)), jnp.tile(sin, (1, reps))
    cos = jnp.concatenate([jnp.ones((ctx_len, LANES), F32), cos], axis=0)
    sin = jnp.concatenate([jnp.zeros((ctx_len, LANES), F32), sin], axis=0)
    return cos, sin


def kernel(x, c, ctx, c_ctx, ada_w, ada_b, norm_mix_g, norm_ffn_g, ab_w_in, ab_w_out, hg_lb_logits,
           hg_norm_g, sw_sink, ga_w_in, ga_w_out, ga_q_norm_g, ga_k_norm_g, peer_w_q, peer_sub_keys,
           peer_u, peer_v, final_norm_g):
    b, l, d = x.shape
    lc = ctx.shape[1]
    assert lc % ROW_BLOCK == 0 and l % ROW_BLOCK == 0 and ada_w.shape[0] == 2
    ctx_blocks = lc // ROW_BLOCK
    h = jnp.concatenate([ctx, x], axis=1)

    n_mod = -(-(b + 1) // SUBLANES) * SUBLANES
    cc = jnp.zeros((n_mod, d), F32).at[:b].set(c).at[b].set(c_ctx)
    mod_all = _ada(cc, ada_w, ada_b)

    def mod_for(layer):
        lat = mod_all[layer, :b].reshape(b, 1, 6, d)
        cx = jnp.broadcast_to(mod_all[layer, b].reshape(1, 1, 6, d), (b, 1, 6, d))
        m = jnp.concatenate([cx, lat], axis=1)
        return jnp.pad(m, ((0, 0), (0, 0), (0, 2), (0, 0)))

    lbs = jnp.cumsum(jax.nn.softmax(hg_lb_logits.astype(F32), axis=0), axis=0)

    m0 = mod_for(0)
    z = _nmm(h, m0, norm_mix_g[0], ab_w_in[0].astype(BF16), row0=0, ctx_blocks=ctx_blocks)
    o = _hgrn(z, lbs[0], lc)
    cos_sw, sin_sw = _rope_tables(l, lc, SW_HEAD_DIM)
    sw = _sw_attention(z, cos_sw, sin_sw, sw_sink[0], lc)
    h = _ab_out(o, z, sw, h, m0, hg_norm_g[0], ab_w_out[0].astype(BF16), ctx_blocks)
    q, f = _nmm(h, m0, norm_ffn_g[0], peer_w_q[0].astype(BF16), row0=3, ctx_blocks=ctx_blocks,
                want_f=True)
    ids, gates = _route(q, peer_sub_keys[0])
    h = _experts(ids, gates, f, h, m0, _pack_experts(peer_u[0], peer_v[0]), final_norm_g,
                 ctx_blocks=ctx_blocks, final=False)

    m1 = mod_for(1)
    z = _nmm(h, m1, norm_mix_g[1], ga_w_in[0].astype(BF16), row0=0, ctx_blocks=ctx_blocks)
    cos_ga, sin_ga = _rope_tables(l, lc, GA_HEAD_DIM)
    qg, kg, vg = _ga_prep(z, cos_ga, sin_ga, ga_q_norm_g[0], ga_k_norm_g[0])
    att = _ga_attention(qg, kg, vg, lc)
    hl = _out_res(att, h, m1, ga_w_out[0].astype(BF16), lc)
    q, f = _nmm(hl, m1, norm_ffn_g[1], peer_w_q[1].astype(BF16), row0=3, ctx_blocks=0, want_f=True)
    ids, gates = _route(q, peer_sub_keys[1])
    return _experts(ids, gates, f, hl, m1, _pack_experts(peer_u[1], peer_v[1]), final_norm_g,
                    ctx_blocks=0, final=True)
```

```python
import functools

import numpy as np
import jax
import jax.numpy as jnp
from jax import lax
from jax.experimental import pallas as pl
from jax.experimental.pallas import tpu as pltpu

F32 = jnp.float32
BF16 = jnp.bfloat16
HIGHEST = lax.Precision.HIGHEST

EPS = 1e-6
NEG_INF = -1e30
ROPE_THETA = 10000.0
GRID_W = 64

HG_HEAD_DIM = 128
HG_HEADS = 4
HG_WIDTH = HG_HEADS * HG_HEAD_DIM
HG_CHUNK = 64
SW_HEAD_DIM = 64
SW_Q_HEADS = 8
SW_KV_HEADS = 2
WINDOW = 128
ATT_BLOCK = 128
GA_HEAD_DIM = 128
GA_Q_HEADS = 8
GA_KV_HEADS = 2
PEER_HEADS = 8
PEER_N_KEYS = 128
PEER_HALF = 128
PEER_TOPK = 16
PEER_PAIRS = PEER_HEADS * PEER_TOPK

LANES = 128
SUBLANES = 8
ROW_BLOCK = 256
VMEM_LIMIT = 56 * 1024 * 1024

_NT = (((1,), (1,)), ((), ()))
_TN = (((0,), (0,)), ((), ()))


def _cparams(*sem):
    return pltpu.CompilerParams(dimension_semantics=sem, vmem_limit_bytes=VMEM_LIMIT)


def _sigmoid(x):
    return 1.0 / (1.0 + jnp.exp(-x))


def _silu(x):
    return x * _sigmoid(x)


def _ada_kernel(c_ref, w_ref, b_ref, o_ref):
    a = _silu(c_ref[...])
    o_ref[0] = jnp.dot(a, w_ref[0], precision=HIGHEST, preferred_element_type=F32) + b_ref[0]


def _ada(cc, ada_w, ada_b):
    depth, d, n = ada_w.shape
    rows = cc.shape[0]
    tn = 1536
    return pl.pallas_call(
        _ada_kernel,
        grid=(depth, n // tn),
        in_specs=[
            pl.BlockSpec((rows, d), lambda l, j: (0, 0)),
            pl.BlockSpec((1, d, tn), lambda l, j: (l, 0, j)),
            pl.BlockSpec((1, 1, tn), lambda l, j: (l, 0, j)),
        ],
        out_specs=pl.BlockSpec((1, rows, tn), lambda l, j: (l, 0, j)),
        out_shape=jax.ShapeDtypeStruct((depth, rows, n), F32),
        compiler_params=_cparams("parallel", "parallel"),
    )(cc, ada_w, ada_b.reshape(depth, 1, n))


def _nmm_kernel(h_ref, m_ref, g_ref, w_ref, o_ref, *f_ref, row0):
    x = h_ref[0]
    ms = jnp.mean(x * x, axis=-1, keepdims=True)
    y = x * lax.rsqrt(ms + EPS) * g_ref[...]
    m = m_ref[0, 0]
    a = (y * (1.0 + m[row0 + 1:row0 + 2]) + m[row0:row0 + 1]).astype(BF16)
    if f_ref:
        f_ref[0][0] = a
    o_ref[0] = jnp.dot(a, w_ref[...], preferred_element_type=F32)


def _nmm(h, mod, g, w, *, row0, ctx_blocks, want_f=False):
    b, r, d = h.shape
    n = w.shape[1]
    tm = ROW_BLOCK
    sel = (lambda i: jnp.minimum(i // ctx_blocks, 1)) if ctx_blocks else (lambda i: 1)
    out_shape = [jax.ShapeDtypeStruct((b, r, n), F32)]
    out_specs = [pl.BlockSpec((1, tm, n), lambda bi, i: (bi, i, 0))]
    if want_f:
        out_shape.append(jax.ShapeDtypeStruct((b, r, d), BF16))
        out_specs.append(pl.BlockSpec((1, tm, d), lambda bi, i: (bi, i, 0)))
    res = pl.pallas_call(
        functools.partial(_nmm_kernel, row0=row0),
        grid=(b, r // tm),
        in_specs=[
            pl.BlockSpec((1, tm, d), lambda bi, i: (bi, i, 0)),
            pl.BlockSpec((1, 1, 8, d), lambda bi, i: (bi, sel(i), 0, 0)),
            pl.BlockSpec((1, d), lambda bi, i: (0, 0)),
            pl.BlockSpec((d, n), lambda bi, i: (0, 0)),
        ],
        out_specs=out_specs,
        out_shape=out_shape,
        compiler_params=_cparams("parallel", "parallel"),
    )(h, mod, g.reshape(1, d), w)
    return res if want_f else res[0]


_HG_LEVELS = (32, 16, 8, 4, 2, 1)
_HG_STACK_ROWS = HG_CHUNK * (2 + 2 * len(_HG_LEVELS)) + SUBLANES


def _hg_tables():
    c = HG_CHUNK
    stacks, masks = [], []
    for rev in (0, 1):
        pos = np.arange(c) if rev == 0 else c - 1 - np.arange(c)
        tt, ss = pos[:, None], pos[None, :]
        rows = [(ss <= tt)]
        qrows, krows, lmask = [], [], []
        for m in _HG_LEVELS:
            same = (tt // (2 * m)) == (ss // (2 * m))
            pt, ps = tt % (2 * m), ss % (2 * m)
            qrows.append(same & (pt >= m) & (ps >= m) & (ps <= pt))
            krows.append(same & (pt < m) & (ps > pt) & (ps <= m - 1))
            lmask.append(same & (pt >= m) & (ps < m))
        lmask.append(tt == ss)
        rows += qrows + krows + [(ss > tt), np.ones((SUBLANES, c), bool)]
        stacks.append(np.tile(np.concatenate(rows, axis=0).astype(np.float32), (1, 3)))
        masks.append(np.stack(lmask).astype(np.float32))
    return np.stack(stacks), np.stack(masks)


def _hg_kernel(q_ref, f_ref, i_ref, lb_ref, w_ref, ml_ref, o_ref, st_ref, *, n_chunks):
    d = pl.program_id(0)
    t = pl.program_id(2)
    c = HG_CHUNK
    nl = len(_HG_LEVELS)

    @pl.when(t == 0)
    def _():
        st_ref[...] = jnp.zeros_like(st_ref)

    lb = lb_ref[...]
    w = w_ref[...]

    def chunk(ci, _):
        cc = jnp.where(d == 0, ci, n_chunks - 1 - ci)
        rows = pl.ds(pl.multiple_of(cc * c, c), c)
        x = f_ref[rows, :]
        k = (1.0 - lb) * _sigmoid(-x)
        lf = jnp.log(1.0 - k)
        q = _silu(q_ref[rows, :])
        v = i_ref[rows, :]
        hi = lf.astype(BF16)
        r1 = lf - hi.astype(F32)
        mid = r1.astype(BF16)
        lo = (r1 - mid.astype(F32)).astype(BF16)
        e = jnp.exp(jnp.dot(w, jnp.concatenate([hi, mid, lo], axis=0), preferred_element_type=F32))
        qe = q * e[0:c]
        kend = k * e[c * (1 + 2 * nl):c * (2 + 2 * nl)]
        tot = e[c * (2 + 2 * nl):c * (2 + 2 * nl) + 1]
        scaled = [(q * e[c * (1 + l):c * (2 + l)], k * e[c * (1 + nl + l):c * (2 + nl + l)])
                  for l in range(nl)]
        outs = []
        for hh in range(HG_HEADS):
            hs = slice(hh * HG_HEAD_DIM, (hh + 1) * HG_HEAD_DIM)
            st = st_ref[hh]
            o = lax.dot_general(qe[:, hs], st, _NT, preferred_element_type=F32)
            a = ml_ref[nl] * lax.dot_general(q[:, hs], k[:, hs], _NT, preferred_element_type=F32)
            for l in range(nl):
                a = a + ml_ref[l] * lax.dot_general(scaled[l][0][:, hs], scaled[l][1][:, hs], _NT,
                                                    preferred_element_type=F32)
            outs.append(o + jnp.dot(a, v[:, hs], preferred_element_type=F32))
            st_ref[hh] = st * tot[:, hs] + lax.dot_general(v[:, hs], kend[:, hs], _TN,
                                                           preferred_element_type=F32)
        o_ref[rows, :] = jnp.concatenate(outs, axis=1)
        return 0

    lax.fori_loop(0, n_chunks, chunk, 0)


def _hgrn(z, lb, ctx_len):
    b, s, _ = z.shape
    tb = ROW_BLOCK
    nt, nctx = s // tb, ctx_len // tb
    stack, masks = _hg_tables()

    def tblk(d, t):
        bwd = jnp.where(t < nctx, nctx - 1 - t, nt - 1 - (t - nctx))
        return jnp.where(d == 0, t, bwd)

    return pl.pallas_call(
        functools.partial(_hg_kernel, n_chunks=tb // HG_CHUNK),
        grid=(2, b, nt),
        in_specs=[
            pl.BlockSpec((None, tb, HG_WIDTH), lambda d, bi, t: (bi, tblk(d, t), 0)),
            pl.BlockSpec((None, tb, HG_WIDTH), lambda d, bi, t: (bi, tblk(d, t), 1 + d)),
            pl.BlockSpec((None, tb, HG_WIDTH), lambda d, bi, t: (bi, tblk(d, t), 3)),
            pl.BlockSpec((1, HG_WIDTH), lambda d, bi, t: (0, 0)),
            pl.BlockSpec((None, _HG_STACK_ROWS, 3 * HG_CHUNK), lambda d, bi, t: (d, 0, 0)),
            pl.BlockSpec((None, len(_HG_LEVELS) + 1, HG_CHUNK, HG_CHUNK),
                         lambda d, bi, t: (d, 0, 0, 0)),
        ],
        out_specs=pl.BlockSpec((None, None, tb, HG_WIDTH), lambda d, bi, t: (d, bi, tblk(d, t), 0)),
        out_shape=jax.ShapeDtypeStruct((2, b, s, HG_WIDTH), F32),
        scratch_shapes=[pltpu.VMEM((HG_HEADS, HG_HEAD_DIM, HG_HEAD_DIM), F32)],
        compiler_params=_cparams("parallel", "parallel", "arbitrary"),
    )(z, z, z, lb.reshape(1, HG_WIDTH), jnp.asarray(stack, BF16), jnp.asarray(masks))


def _rope_half(x, cos, sin_signed, half):
    width = x.shape[-1]
    if 2 * half == width:
        sw = pltpu.roll(x, half, 1)
    else:
        lane = lax.broadcasted_iota(jnp.int32, x.shape, 1)
        sw = jnp.where(lane % (2 * half) < half, pltpu.roll(x, width - half, 1), pltpu.roll(x, half, 1))
    return x * cos + sw * sin_signed


def _sw_kernel(q_ref, kc_ref, vc_ref, kp_ref, k0_ref, kn_ref, vp_ref, v0_ref, vn_ref,
               cq_ref, sq_ref, cp_ref, sp_ref, cn_ref, sn_ref, sink_ref, o_ref, *, ctx_len, seq):
    jb = pl.program_id(1)
    blk = ATT_BLOCK
    half = SW_HEAD_DIM // 2
    cq, sq = cq_ref[...], sq_ref[...]
    reps = q_ref.shape[-1] // LANES
    q = _rope_half(q_ref[0], jnp.concatenate([cq] * reps, axis=1),
                   jnp.concatenate([sq] * reps, axis=1), half)
    kw = jnp.concatenate([kp_ref[0], k0_ref[0], kn_ref[0]], axis=0)
    cw = jnp.concatenate([cp_ref[...], cq, cn_ref[...]], axis=0)
    sw = jnp.concatenate([sp_ref[...], sq, sn_ref[...]], axis=0)
    kw = _rope_half(kw, cw, sw, half)
    k_all = jnp.concatenate([kc_ref[0], kw], axis=0).astype(BF16)
    v_all = jnp.concatenate([vc_ref[0], vp_ref[0], v0_ref[0], vn_ref[0]], axis=0).astype(BF16)
    nk = ctx_len + 3 * blk
    col = lax.broadcasted_iota(jnp.int32, (blk, nk), 1)
    trow = jb * blk + lax.broadcasted_iota(jnp.int32, (blk, nk), 0)
    srow = (jb - 1) * blk + (col - ctx_len)
    band = ((trow >= ctx_len) & (srow >= ctx_len) & (srow < seq) & (jnp.abs(trow - srow) <= WINDOW))
    mask = (col < ctx_len) | band
    scale = SW_HEAD_DIM ** -0.5
    group = SW_Q_HEADS // SW_KV_HEADS
    outs = []
    for hq in range(SW_Q_HEADS):
        g = hq // group
        qh = q[:, hq * SW_HEAD_DIM:(hq + 1) * SW_HEAD_DIM].astype(BF16)
        kh = k_all[:, g * SW_HEAD_DIM:(g + 1) * SW_HEAD_DIM]
        vh = v_all[:, g * SW_HEAD_DIM:(g + 1) * SW_HEAD_DIM]
        s = lax.dot_general(qh, kh, _NT, preferred_element_type=F32) * scale
        s = jnp.where(mask, s, NEG_INF)
        sk = sink_ref[hq]
        m = jnp.maximum(jnp.max(s, axis=-1, keepdims=True), sk)
        e = jnp.exp(s - m)
        p = e / (jnp.sum(e, axis=-1, keepdims=True) + jnp.exp(sk - m))
        outs.append(jnp.dot(p.astype(BF16), vh, preferred_element_type=F32))
    o_ref[0] = jnp.concatenate(outs, axis=1)


def _sw_attention(z, cos, sin, sink, ctx_len):
    b, s, _ = z.shape
    blk = ATT_BLOCK
    nb = s // blk
    qw = SW_Q_HEADS * SW_HEAD_DIM
    qcol, kcol, vcol = 5 * HG_WIDTH // qw, 5 * HG_WIDTH // LANES + 4, 5 * HG_WIDTH // LANES + 5
    prev = lambda j: jnp.maximum(j - 1, 0)
    nxt = lambda j: jnp.minimum(j + 1, nb - 1)
    kv = lambda col, f: pl.BlockSpec((1, blk, LANES), lambda bi, j: (bi, f(j), col))
    tab = lambda f: pl.BlockSpec((blk, LANES), lambda bi, j: (f(j), 0))
    same = lambda j: j
    return pl.pallas_call(
        functools.partial(_sw_kernel, ctx_len=ctx_len, seq=s),
        grid=(b, nb),
        in_specs=[
            pl.BlockSpec((1, blk, qw), lambda bi, j: (bi, j, qcol)),
            pl.BlockSpec((1, ctx_len, LANES), lambda bi, j: (bi, 0, kcol)),
            pl.BlockSpec((1, ctx_len, LANES), lambda bi, j: (bi, 0, vcol)),
            kv(kcol, prev), kv(kcol, same), kv(kcol, nxt),
            kv(vcol, prev), kv(vcol, same), kv(vcol, nxt),
            tab(same), tab(same), tab(prev), tab(prev), tab(nxt), tab(nxt),
            pl.BlockSpec(memory_space=pltpu.SMEM),
        ],
        out_specs=pl.BlockSpec((1, blk, qw), lambda bi, j: (bi, j, 0)),
        out_shape=jax.ShapeDtypeStruct((b, s, qw), F32),
        compiler_params=_cparams("parallel", "parallel"),
    )(z, z, z, z, z, z, z, z, z, cos, sin, cos, sin, cos, sin, sink)


def _ab_out_kernel(of_ref, ob_ref, g_ref, sw_ref, h_ref, m_ref, ng_ref, w_ref, o_ref):
    o = of_ref[...] + ob_ref[...]
    ng = ng_ref[...]
    parts = []
    for hh in range(HG_HEADS):
        oh = o[:, hh * HG_HEAD_DIM:(hh + 1) * HG_HEAD_DIM]
        ms = jnp.mean(oh * oh, axis=-1, keepdims=True)
        parts.append(oh * lax.rsqrt(ms + EPS) * ng)
    hg = jnp.concatenate(parts, axis=1) * _silu(g_ref[0])
    mix = jnp.concatenate([hg, sw_ref[0]], axis=1).astype(BF16)
    y = jnp.dot(mix, w_ref[...], preferred_element_type=F32)
    o_ref[0] = h_ref[0] + m_ref[0, 0][2:3] * y


def _ab_out(o, z, sw, h, mod, ng, w, ctx_blocks):
    b, s, d = h.shape
    tm = ROW_BLOCK
    sel = lambda i: jnp.minimum(i // ctx_blocks, 1)
    return pl.pallas_call(
        _ab_out_kernel,
        grid=(b, s // tm),
        in_specs=[
            pl.BlockSpec((None, None, tm, HG_WIDTH), lambda bi, i: (0, bi, i, 0)),
            pl.BlockSpec((None, None, tm, HG_WIDTH), lambda bi, i: (1, bi, i, 0)),
            pl.BlockSpec((1, tm, HG_WIDTH), lambda bi, i: (bi, i, 4)),
            pl.BlockSpec((1, tm, sw.shape[-1]), lambda bi, i: (bi, i, 0)),
            pl.BlockSpec((1, tm, d), lambda bi, i: (bi, i, 0)),
            pl.BlockSpec((1, 1, 8, d), lambda bi, i: (bi, sel(i), 0, 0)),
            pl.BlockSpec((1, HG_HEAD_DIM), lambda bi, i: (0, 0)),
            pl.BlockSpec(w.shape, lambda bi, i: (0, 0)),
        ],
        out_specs=pl.BlockSpec((1, tm, d), lambda bi, i: (bi, i, 0)),
        out_shape=jax.ShapeDtypeStruct((b, s, d), F32),
        compiler_params=_cparams("parallel", "parallel"),
    )(o, o, z, sw, h, mod, ng.reshape(1, HG_HEAD_DIM), w)


def _ga_prep_kernel(z_ref, cos_ref, sin_ref, qg_ref, kg_ref, q_ref, k_ref, v_ref):
    z = z_ref[0]
    cos, sin = cos_ref[...], sin_ref[...]
    hd = GA_HEAD_DIM

    def norm_rope(x, g):
        ms = jnp.mean(x * x, axis=-1, keepdims=True)
        return _rope_half(x * lax.rsqrt(ms + EPS) * g, cos, sin, hd // 2)

    qg, kg = qg_ref[...], kg_ref[...]
    qs = [norm_rope(z[:, i * hd:(i + 1) * hd], qg) for i in range(GA_Q_HEADS)]
    q_ref[0] = jnp.concatenate(qs, axis=1).astype(BF16)
    k0 = GA_Q_HEADS * hd
    ks = [norm_rope(z[:, k0 + i * hd:k0 + (i + 1) * hd], kg) for i in range(GA_KV_HEADS)]
    k_ref[0] = jnp.concatenate(ks, axis=1).astype(BF16)
    v_ref[0] = z[:, k0 + GA_KV_HEADS * hd:].astype(BF16)


def _ga_prep(z, cos, sin, qg, kg):
    b, s, n = z.shape
    tm = ROW_BLOCK
    qw, kw = GA_Q_HEADS * GA_HEAD_DIM, GA_KV_HEADS * GA_HEAD_DIM
    row = lambda width: pl.BlockSpec((1, tm, width), lambda bi, i: (bi, i, 0))
    return pl.pallas_call(
        _ga_prep_kernel,
        grid=(b, s // tm),
        in_specs=[
            row(n),
            pl.BlockSpec((tm, GA_HEAD_DIM), lambda bi, i: (i, 0)),
            pl.BlockSpec((tm, GA_HEAD_DIM), lambda bi, i: (i, 0)),
            pl.BlockSpec((1, GA_HEAD_DIM), lambda bi, i: (0, 0)),
            pl.BlockSpec((1, GA_HEAD_DIM), lambda bi, i: (0, 0)),
        ],
        out_specs=[row(qw), row(kw), row(kw)],
        out_shape=[jax.ShapeDtypeStruct((b, s, qw), BF16),
                   jax.ShapeDtypeStruct((b, s, kw), BF16),
                   jax.ShapeDtypeStruct((b, s, kw), BF16)],
        compiler_params=_cparams("parallel", "parallel"),
    )(z, cos, sin, qg.reshape(1, -1), kg.reshape(1, -1))


_GA_TQ = 128
_GA_KC = 256


def _flash_kernel(q_ref, k_ref, v_ref, o_ref, s_ref):
    hd = GA_HEAD_DIM
    group = GA_Q_HEADS // GA_KV_HEADS
    q = q_ref[0]
    qs = jnp.concatenate([q[:, i * hd:(i + 1) * hd] for i in range(group)], axis=0)
    scale = hd ** -0.5
    n_chunks = k_ref.shape[1] // _GA_KC
    lane_tiles = _GA_KC // LANES
    mrun = None
    for c in range(n_chunks):
        s = lax.dot_general(qs, k_ref[0, c * _GA_KC:(c + 1) * _GA_KC, :], _NT,
                            preferred_element_type=F32) * scale
        s_ref[:, c * _GA_KC:(c + 1) * _GA_KC] = s
        for j in range(lane_tiles):
            sj = s[:, j * LANES:(j + 1) * LANES]
            mrun = sj if mrun is None else jnp.maximum(mrun, sj)
    m = jnp.broadcast_to(jnp.max(mrun, axis=-1, keepdims=True), mrun.shape)
    lrun = jnp.zeros_like(mrun)
    acc = jnp.zeros((qs.shape[0], hd), F32)
    for c in range(n_chunks):
        ps = [jnp.exp(s_ref[:, c * _GA_KC + j * LANES:c * _GA_KC + (j + 1) * LANES] - m)
              for j in range(lane_tiles)]
        for pj in ps:
            lrun = lrun + pj
        p = jnp.concatenate(ps, axis=1).astype(BF16)
        acc = acc + jnp.dot(p, v_ref[0, c * _GA_KC:(c + 1) * _GA_KC, :], preferred_element_type=F32)
    o = acc / jnp.sum(lrun, axis=-1, keepdims=True)
    tq = q.shape[0]
    o_ref[0] = jnp.concatenate([o[i * tq:(i + 1) * tq] for i in range(group)], axis=1).astype(o_ref.dtype)


def _ga_attention(q, k, v, ctx_len):
    b, s, qw = q.shape
    l = s - ctx_len
    tq = _GA_TQ
    group = GA_Q_HEADS // GA_KV_HEADS
    gw = group * GA_HEAD_DIM
    off = ctx_len // tq
    return pl.pallas_call(
        _flash_kernel,
        grid=(b, GA_KV_HEADS, l // tq),
        in_specs=[
            pl.BlockSpec((1, tq, gw), lambda bi, g, i: (bi, off + i, g)),
            pl.BlockSpec((1, s, GA_HEAD_DIM), lambda bi, g, i: (bi, 0, g)),
            pl.BlockSpec((1, s, GA_HEAD_DIM), lambda bi, g, i: (bi, 0, g)),
        ],
        out_specs=pl.BlockSpec((1, tq, gw), lambda bi, g, i: (bi, i, g)),
        out_shape=jax.ShapeDtypeStruct((b, l, qw), BF16),
        scratch_shapes=[pltpu.VMEM((group * tq, s), F32)],
        compiler_params=_cparams("parallel", "parallel", "parallel"),
    )(q, k, v)


def _out_res_kernel(a_ref, h_ref, m_ref, w_ref, o_ref):
    y = jnp.dot(a_ref[0], w_ref[...], preferred_element_type=F32)
    o_ref[0] = h_ref[0] + m_ref[0, 0][2:3] * y


def _out_res(a, h, mod, w, ctx_len):
    b, l, n = a.shape
    d = h.shape[-1]
    tm = ROW_BLOCK
    off = ctx_len // tm
    return pl.pallas_call(
        _out_res_kernel,
        grid=(b, l // tm),
        in_specs=[
            pl.BlockSpec((1, tm, n), lambda bi, i: (bi, i, 0)),
            pl.BlockSpec((1, tm, d), lambda bi, i: (bi, off + i, 0)),
            pl.BlockSpec((1, 1, 8, d), lambda bi, i: (bi, 1, 0, 0)),
            pl.BlockSpec(w.shape, lambda bi, i: (0, 0)),
        ],
        out_specs=pl.BlockSpec((1, tm, d), lambda bi, i: (bi, i, 0)),
        out_shape=jax.ShapeDtypeStruct((b, l, d), F32),
        compiler_params=_cparams("parallel", "parallel"),
    )(a, h, mod, w)


_NO_RANK = 1 << 20


def _topk_rows(s, rank=None, payload=None):
    if rank is None:
        rank = lax.broadcasted_iota(jnp.int32, s.shape, 0)
    vals, sel = [], []
    for _ in range(PEER_TOPK):
        m = jnp.max(s, axis=0, keepdims=True)
        i = jnp.min(jnp.where(s == m, rank, _NO_RANK), axis=0, keepdims=True)
        hit = rank == i
        vals.append(m)
        sel.append(i if payload is None else jnp.max(jnp.where(hit, payload, -1), axis=0, keepdims=True))
        s = jnp.where(hit, -jnp.inf, s)
    return jnp.concatenate(vals, axis=0), jnp.concatenate(sel, axis=0)


def _cand_slabs():
    k = PEER_TOPK
    slabs = [('b', 0, 0), ('b', 0, 8)] + [('b', a, 0) for a in range(1, 8)] + [('a', 8, 0)]
    rank = []
    for kind, a, b in slabs:
        for j in range(SUBLANES):
            aa, bb = (a, b + j) if kind == 'b' else (a + j, b)
            rank.append(aa * k + bb if (aa + 1) * (bb + 1) <= k else _NO_RANK)
    return slabs, np.asarray(rank, np.int32)


def _route_kernel(q_ref, keys_ref, rank_ref, ids_ref, gate_ref):
    q = q_ref[0]
    slabs, _ = _cand_slabs()
    rank = jnp.concatenate([rank_ref[...]] * (q.shape[0] // LANES), axis=1)
    ids, gates = [], []
    for h in range(PEER_HEADS):
        tops = []
        for p in range(2):
            c0 = (h * 2 + p) * PEER_HALF
            st = lax.dot_general(keys_ref[h, p], q[:, c0:c0 + PEER_HALF], _NT,
                                 precision=HIGHEST, preferred_element_type=F32)
            tops.append(_topk_rows(st))
        (s1, i1), (s2, i2) = tops
        pick = lambda x, y, kind, a, b: ((x[a:a + 1], y[b:b + SUBLANES]) if kind == 'b'
                                         else (x[a:a + SUBLANES], y[b:b + 1]))
        cand = jnp.concatenate([sum(pick(s1, s2, *sl)) for sl in slabs], axis=0)
        cand = jnp.where(rank < _NO_RANK, cand, -jnp.inf)
        eid = jnp.concatenate([(lambda x, y: x * PEER_N_KEYS + y)(*pick(i1, i2, *sl)) for sl in slabs],
                              axis=0)
        sc, e = _topk_rows(cand, rank, eid)
        ex = jnp.exp(sc - jnp.max(sc, axis=0, keepdims=True))
        gates.append(ex / jnp.sum(ex, axis=0, keepdims=True))
        ids.append(e)
    ids_ref[0] = jnp.concatenate(ids, axis=0).astype(F32).T.astype(jnp.int32)
    gate_ref[0] = jnp.concatenate(gates, axis=0).T


def _route(q, keys):
    b, r, n = q.shape
    tm = ROW_BLOCK
    rank = np.tile(_cand_slabs()[1][:, None], (1, LANES))
    return pl.pallas_call(
        _route_kernel,
        grid=(b, r // tm),
        in_specs=[
            pl.BlockSpec((1, tm, n), lambda bi, i: (bi, i, 0)),
            pl.BlockSpec(keys.shape, lambda bi, i: (0, 0, 0, 0)),
            pl.BlockSpec(rank.shape, lambda bi, i: (0, 0)),
        ],
        out_specs=[pl.BlockSpec((1, tm, PEER_PAIRS), lambda bi, i: (bi, i, 0)),
                   pl.BlockSpec((1, tm, PEER_PAIRS), lambda bi, i: (bi, i, 0))],
        out_shape=[jax.ShapeDtypeStruct((b, r, PEER_PAIRS), jnp.int32),
                   jax.ShapeDtypeStruct((b, r, PEER_PAIRS), F32)],
        compiler_params=_cparams("parallel", "parallel"),
    )(q, keys, jnp.asarray(rank))


_EX_TOK = 8
_EX_HALVES = 4
_EX_AHEAD = 2
_EX_NP = _EX_TOK * PEER_PAIRS
_ROW_TILES = 8
_ROW_PITCH = 12
_EX_GROUPS = 2 * _ROW_TILES


def _expert_kernel(ids_ref, idsn_ref, f_ref, gate_ref, h_ref, m_ref, fg_ref, ex_ref, uv_ref, o_ref,
                   *scratch, n_steps, final):
    bufs, sem = scratch[:_EX_HALVES], scratch[_EX_HALVES]
    step = pl.program_id(0) * pl.num_programs(1) + pl.program_id(1)
    per_group = _EX_NP // _EX_GROUPS

    def start_copy(e, dst, p, sem_i, priority):
        pltpu.make_async_copy(uv_ref.at[e], dst.at[pl.ds(p * _ROW_PITCH, _ROW_TILES), :],
                              sem.at[sem_i]).start(priority=priority)

    def wait_all(dst, sem_i):
        rows = pl.ds(0, _EX_NP * _ROW_TILES)
        pltpu.make_async_copy(dst.at[rows, :], dst.at[rows, :], sem.at[sem_i]).wait()

    def half(cur, half_i, issue_ids, issue_off, other, other_sem):
        rows = slice(half_i * _EX_TOK, (half_i + 1) * _EX_TOK)
        f = f_ref[0, rows, :]

        def issue(g):
            for u in range(per_group):
                p = g * per_group + u
                start_copy(issue_ids[0, issue_off + p], other, p, other_sem, u % 2)

        words = lambda r: pltpu.bitcast(cur[pl.ds(r, _EX_NP, stride=_ROW_PITCH), :], BF16)
        a = jnp.zeros((_EX_TOK, _EX_NP), F32)
        for r in range(_ROW_TILES):
            issue(r)
            u = pltpu.bitcast(lax.shift_left(cur[pl.ds(r, _EX_NP, stride=_ROW_PITCH), :], 16), F32)
            a = a + lax.dot_general(f[:, r * LANES:(r + 1) * LANES], u.astype(BF16), _NT,
                                    preferred_element_type=F32)
        act = 0.5 * a * (1.0 + lax.erf(a * (2.0 ** -0.5)))
        gate = jnp.concatenate([gate_ref[0, rows, :]] * _EX_TOK, axis=1)
        own = (lax.broadcasted_iota(jnp.int32, a.shape, 1) // PEER_PAIRS
               == lax.broadcasted_iota(jnp.int32, a.shape, 0))
        ga = jnp.where(own, gate * act, 0.0).astype(BF16)
        coef = jnp.concatenate(
            [jnp.dot(ga[:, t * PEER_PAIRS:(t + 1) * PEER_PAIRS], ex_ref[...], preferred_element_type=F32)
             for t in range(_EX_TOK)], axis=1).astype(BF16)
        ys = []
        for r in range(_ROW_TILES):
            issue(_ROW_TILES + r)
            ys.append(jnp.dot(coef, words(r), preferred_element_type=F32))
        hn = h_ref[0, rows, :] + m_ref[0, 0][5:6] * jnp.concatenate(ys, axis=1)
        if final:
            ms = jnp.mean(hn * hn, axis=-1, keepdims=True)
            hn = hn * lax.rsqrt(ms + EPS) * fg_ref[...]
        o_ref[0, rows, :] = hn

    @pl.when(step == 0)
    def _():
        for k in range(_EX_AHEAD):
            def body(p, _, k=k):
                start_copy(ids_ref[0, k * _EX_NP + p], bufs[k], p, k, 0)
                return 0
            lax.fori_loop(0, _EX_NP, body, 0)

    for k in range(_EX_HALVES):
        wait_all(bufs[k], k)
        tgt = k + _EX_AHEAD
        if tgt < _EX_HALVES:
            half(bufs[k], k, ids_ref, tgt * _EX_NP, bufs[tgt], tgt)
        else:
            half(bufs[k], k, idsn_ref, (tgt - _EX_HALVES) * _EX_NP, bufs[tgt - _EX_HALVES],
                 tgt - _EX_HALVES)

    @pl.when(step == n_steps - 1)
    def _():
        for k in range(_EX_AHEAD):
            wait_all(bufs[k], k)


def _experts(ids, gates, f, h, mod, uv, final_g, *, ctx_blocks, final):
    b, r, d = h.shape
    tok = _EX_HALVES * _EX_TOK
    nb = r // tok
    n_steps = b * nb
    ids_flat = ids.reshape(n_steps, 1, tok * PEER_PAIRS)
    per_ctx = ctx_blocks * (ROW_BLOCK // tok)
    sel = (lambda i: jnp.minimum(i // per_ctx, 1)) if ctx_blocks else (lambda i: 1)
    blk = lambda width: pl.BlockSpec((1, tok, width), lambda bi, i: (bi, i, 0))
    lin = lambda bi, i: bi * nb + i
    expand = np.zeros((PEER_PAIRS, 2 * PEER_PAIRS), np.float32)
    expand[np.arange(PEER_PAIRS), 2 * np.arange(PEER_PAIRS) + 1] = 1.0
    buf = pltpu.VMEM((_EX_NP * _ROW_PITCH, LANES), jnp.int32)
    return pl.pallas_call(
        functools.partial(_expert_kernel, n_steps=n_steps, final=final),
        grid=(b, nb),
        in_specs=[
            pl.BlockSpec((None, 1, tok * PEER_PAIRS), lambda bi, i: (lin(bi, i), 0, 0),
                         memory_space=pltpu.SMEM),
            pl.BlockSpec((None, 1, tok * PEER_PAIRS),
                         lambda bi, i: (jnp.minimum(lin(bi, i) + 1, n_steps - 1), 0, 0),
                         memory_space=pltpu.SMEM),
            blk(d), blk(PEER_PAIRS), blk(d),
            pl.BlockSpec((1, 1, 8, d), lambda bi, i: (bi, sel(i), 0, 0)),
            pl.BlockSpec((1, d), lambda bi, i: (0, 0)),
            pl.BlockSpec(expand.shape, lambda bi, i: (0, 0)),
            pl.BlockSpec(memory_space=pl.ANY),
        ],
        out_specs=blk(d),
        out_shape=jax.ShapeDtypeStruct((b, r, d), F32),
        scratch_shapes=[buf] * _EX_HALVES + [pltpu.SemaphoreType.DMA((_EX_HALVES,))],
        compiler_params=_cparams("arbitrary", "arbitrary"),
    )(ids_flat, ids_flat, f, gates, h, mod, final_g.reshape(1, d), jnp.asarray(expand, BF16), uv)


def _pack_experts(u, v):
    ub = lax.bitcast_convert_type(u.astype(BF16), jnp.uint16).astype(jnp.uint32)
    vb = lax.bitcast_convert_type(v.astype(BF16), jnp.uint16).astype(jnp.uint32)
    words = lax.bitcast_convert_type(ub | (vb << 16), jnp.int32)
    return words.reshape(u.shape[0], _ROW_TILES, LANES)


def _rope_tables(seq, ctx_len, head_dim):
    n_rows = seq // GRID_W
    row = jnp.repeat(jnp.arange(n_rows), GRID_W).astype(F32)
    col = jnp.tile(jnp.arange(GRID_W), n_rows).astype(F32)
    n_freq = head_dim // 4
    inv = ROPE_THETA ** (-jnp.arange(n_freq, dtype=F32) / n_freq)
    ang = jnp.concatenate([row[:, None] * inv, col[:, None] * inv], axis=-1)
    cos = jnp.concatenate([jnp.cos(ang), jnp.cos(ang)], axis=-1)
    sin = jnp.concatenate([-jnp.sin(ang), jnp.sin(ang)], axis=-1)
    reps = LANES // head_dim
    cos, sin = jnp.tile(cos, (1, reps)), jnp.tile(sin, (1, reps))
    cos = jnp.concatenate([jnp.ones((ctx_len, LANES), F32), cos], axis=0)
    sin = jnp.concatenate([jnp.zeros((ctx_len, LANES), F32), sin], axis=0)
    return cos, sin


def kernel(x, c, ctx, c_ctx, ada_w, ada_b, norm_mix_g, norm_ffn_g, ab_w_in, ab_w_out, hg_lb_logits,
           hg_norm_g, sw_sink, ga_w_in, ga_w_out, ga_q_norm_g, ga_k_norm_g, peer_w_q, peer_sub_keys,
           peer_u, peer_v, final_norm_g):
    b, l, d = x.shape
    lc = ctx.shape[1]
    assert lc % ROW_BLOCK == 0 and l % ROW_BLOCK == 0 and ada_w.shape[0] == 2
    ctx_blocks = lc // ROW_BLOCK
    h = jnp.concatenate([ctx, x], axis=1)

    n_mod = -(-(b + 1) // SUBLANES) * SUBLANES
    cc = jnp.zeros((n_mod, d), F32).at[:b].set(c).at[b].set(c_ctx)
    mod_all = _ada(cc, ada_w, ada_b)

    def mod_for(layer):
        lat = mod_all[layer, :b].reshape(b, 1, 6, d)
        cx = jnp.broadcast_to(mod_all[layer, b].reshape(1, 1, 6, d), (b, 1, 6, d))
        m = jnp.concatenate([cx, lat], axis=1)
        return jnp.pad(m, ((0, 0), (0, 0), (0, 2), (0, 0)))

    lbs = jnp.cumsum(jax.nn.softmax(hg_lb_logits.astype(F32), axis=0), axis=0)

    m0 = mod_for(0)
    z = _nmm(h, m0, norm_mix_g[0], ab_w_in[0].astype(BF16), row0=0, ctx_blocks=ctx_blocks)
    o = _hgrn(z, lbs[0], lc)
    cos_sw, sin_sw = _rope_tables(l, lc, SW_HEAD_DIM)
    sw = _sw_attention(z, cos_sw, sin_sw, sw_sink[0], lc)
    h = _ab_out(o, z, sw, h, m0, hg_norm_g[0], ab_w_out[0].astype(BF16), ctx_blocks)
    q, f = _nmm(h, m0, norm_ffn_g[0], peer_w_q[0].astype(BF16), row0=3, ctx_blocks=ctx_blocks,
                want_f=True)
    ids, gates = _route(q, peer_sub_keys[0])
    h = _experts(ids, gates, f, h, m0, _pack_experts(peer_u[0], peer_v[0]), final_norm_g,
                 ctx_blocks=ctx_blocks, final=False)

    m1 = mod_for(1)
    z = _nmm(h, m1, norm_mix_g[1], ga_w_in[0].astype(BF16), row0=0, ctx_blocks=ctx_blocks)
    cos_ga, sin_ga = _rope_tables(l, lc, GA_HEAD_DIM)
    qg, kg, vg = _ga_prep(z, cos_ga, sin_ga, ga_q_norm_g[0], ga_k_norm_g[0])
    att = _ga_attention(qg, kg, vg, lc)
    hl = _out_res(att, h, m1, ga_w_out[0].astype(BF16), lc)
    q, f = _nmm(hl, m1, norm_ffn_g[1], peer_w_q[1].astype(BF16), row0=3, ctx_blocks=0, want_f=True)
    ids, gates = _route(q, peer_sub_keys[1])
    return _experts(ids, gates, f, hl, m1, _pack_experts(peer_u[1], peer_v[1]), final_norm_g,
                    ctx_blocks=0, final=True)
```

```python
import functools

import numpy as np
import jax
import jax.numpy as jnp
from jax import lax
from jax.experimental import pallas as pl
from jax.experimental.pallas import tpu as pltpu

F32 = jnp.float32
BF16 = jnp.bfloat16
HIGHEST = lax.Precision.HIGHEST

EPS = 1e-6
NEG_INF = -1e30
ROPE_THETA = 10000.0
GRID_W = 64

HG_HEAD_DIM = 128
HG_HEADS = 4
HG_WIDTH = HG_HEADS * HG_HEAD_DIM
HG_CHUNK = 64
SW_HEAD_DIM = 64
SW_Q_HEADS = 8
SW_KV_HEADS = 2
WINDOW = 128
ATT_BLOCK = 128
GA_HEAD_DIM = 128
GA_Q_HEADS = 8
GA_KV_HEADS = 2
PEER_HEADS = 8
PEER_N_KEYS = 128
PEER_HALF = 128
PEER_TOPK = 16
PEER_PAIRS = PEER_HEADS * PEER_TOPK

LANES = 128
SUBLANES = 8
ROW_BLOCK = 256
VMEM_LIMIT = 56 * 1024 * 1024

_NT = (((1,), (1,)), ((), ()))
_TN = (((0,), (0,)), ((), ()))


def _cparams(*sem):
    return pltpu.CompilerParams(dimension_semantics=sem, vmem_limit_bytes=VMEM_LIMIT)


def _sigmoid(x):
    return 1.0 / (1.0 + jnp.exp(-x))


def _silu(x):
    return x * _sigmoid(x)


def _ada_kernel(c_ref, w_ref, b_ref, o_ref):
    a = _silu(c_ref[...])
    o_ref[0] = jnp.dot(a, w_ref[0], precision=HIGHEST, preferred_element_type=F32) + b_ref[0]


def _ada(cc, ada_w, ada_b):
    depth, d, n = ada_w.shape
    rows = cc.shape[0]
    tn = 1536
    return pl.pallas_call(
        _ada_kernel,
        grid=(depth, n // tn),
        in_specs=[
            pl.BlockSpec((rows, d), lambda l, j: (0, 0)),
            pl.BlockSpec((1, d, tn), lambda l, j: (l, 0, j)),
            pl.BlockSpec((1, 1, tn), lambda l, j: (l, 0, j)),
        ],
        out_specs=pl.BlockSpec((1, rows, tn), lambda l, j: (l, 0, j)),
        out_shape=jax.ShapeDtypeStruct((depth, rows, n), F32),
        compiler_params=_cparams("parallel", "parallel"),
    )(cc, ada_w, ada_b.reshape(depth, 1, n))


def _nmm_kernel(h_ref, m_ref, g_ref, w_ref, o_ref, *f_ref, row0):
    x = h_ref[0]
    ms = jnp.mean(x * x, axis=-1, keepdims=True)
    y = x * lax.rsqrt(ms + EPS) * g_ref[...]
    m = m_ref[0, 0]
    a = (y * (1.0 + m[row0 + 1:row0 + 2]) + m[row0:row0 + 1]).astype(BF16)
    if f_ref:
        f_ref[0][0] = a
    o_ref[0] = jnp.dot(a, w_ref[...], preferred_element_type=F32)


def _nmm(h, mod, g, w, *, row0, ctx_blocks, want_f=False):
    b, r, d = h.shape
    n = w.shape[1]
    tm = ROW_BLOCK
    sel = (lambda i: jnp.minimum(i // ctx_blocks, 1)) if ctx_blocks else (lambda i: 1)
    out_shape = [jax.ShapeDtypeStruct((b, r, n), F32)]
    out_specs = [pl.BlockSpec((1, tm, n), lambda bi, i: (bi, i, 0))]
    if want_f:
        out_shape.append(jax.ShapeDtypeStruct((b, r, d), BF16))
        out_specs.append(pl.BlockSpec((1, tm, d), lambda bi, i: (bi, i, 0)))
    res = pl.pallas_call(
        functools.partial(_nmm_kernel, row0=row0),
        grid=(b, r // tm),
        in_specs=[
            pl.BlockSpec((1, tm, d), lambda bi, i: (bi, i, 0)),
            pl.BlockSpec((1, 1, 8, d), lambda bi, i: (bi, sel(i), 0, 0)),
            pl.BlockSpec((1, d), lambda bi, i: (0, 0)),
            pl.BlockSpec((d, n), lambda bi, i: (0, 0)),
        ],
        out_specs=out_specs,
        out_shape=out_shape,
        compiler_params=_cparams("parallel", "parallel"),
    )(h, mod, g.reshape(1, d), w)
    return res if want_f else res[0]


_HG_LEVELS = (32, 16, 8, 4, 2, 1)
_HG_STACK_ROWS = HG_CHUNK * (2 + 2 * len(_HG_LEVELS)) + SUBLANES


def _hg_tables():
    c = HG_CHUNK
    stacks, masks = [], []
    for rev in (0, 1):
        pos = np.arange(c) if rev == 0 else c - 1 - np.arange(c)
        tt, ss = pos[:, None], pos[None, :]
        rows = [(ss <= tt)]
        qrows, krows, lmask = [], [], []
        for m in _HG_LEVELS:
            same = (tt // (2 * m)) == (ss // (2 * m))
            pt, ps = tt % (2 * m), ss % (2 * m)
            qrows.append(same & (pt >= m) & (ps >= m) & (ps <= pt))
            krows.append(same & (pt < m) & (ps > pt) & (ps <= m - 1))
            lmask.append(same & (pt >= m) & (ps < m))
        lmask.append(tt == ss)
        rows += qrows + krows + [(ss > tt), np.ones((SUBLANES, c), bool)]
        stacks.append(np.tile(np.concatenate(rows, axis=0).astype(np.float32), (1, 3)))
        masks.append(np.stack(lmask).astype(np.float32))
    return np.stack(stacks), np.stack(masks)


def _hg_kernel(q_ref, f_ref, i_ref, lb_ref, w_ref, ml_ref, o_ref, st_ref, *, n_chunks):
    d = pl.program_id(0)
    t = pl.program_id(2)
    c = HG_CHUNK
    nl = len(_HG_LEVELS)

    @pl.when(t == 0)
    def _():
        st_ref[...] = jnp.zeros_like(st_ref)

    lb = lb_ref[...]
    w = w_ref[...]

    def chunk(ci, _):
        cc = jnp.where(d == 0, ci, n_chunks - 1 - ci)
        rows = pl.ds(pl.multiple_of(cc * c, c), c)
        x = f_ref[rows, :]
        k = (1.0 - lb) * _sigmoid(-x)
        lf = jnp.log(1.0 - k)
        q = _silu(q_ref[rows, :])
        v = i_ref[rows, :]
        hi = lf.astype(BF16)
        r1 = lf - hi.astype(F32)
        mid = r1.astype(BF16)
        lo = (r1 - mid.astype(F32)).astype(BF16)
        e = jnp.exp(jnp.dot(w, jnp.concatenate([hi, mid, lo], axis=0), preferred_element_type=F32))
        qe = q * e[0:c]
        kend = k * e[c * (1 + 2 * nl):c * (2 + 2 * nl)]
        tot = e[c * (2 + 2 * nl):c * (2 + 2 * nl) + 1]
        scaled = [(q * e[c * (1 + l):c * (2 + l)], k * e[c * (1 + nl + l):c * (2 + nl + l)])
                  for l in range(nl)]
        outs = []
        for hh in range(HG_HEADS):
            hs = slice(hh * HG_HEAD_DIM, (hh + 1) * HG_HEAD_DIM)
            st = st_ref[hh]
            o = lax.dot_general(qe[:, hs], st, _NT, preferred_element_type=F32)
            a = ml_ref[nl] * lax.dot_general(q[:, hs], k[:, hs], _NT, preferred_element_type=F32)
            for l in range(nl):
                a = a + ml_ref[l] * lax.dot_general(scaled[l][0][:, hs], scaled[l][1][:, hs], _NT,
                                                    preferred_element_type=F32)
            outs.append(o + jnp.dot(a, v[:, hs], preferred_element_type=F32))
            st_ref[hh] = st * tot[:, hs] + lax.dot_general(v[:, hs], kend[:, hs], _TN,
                                                           preferred_element_type=F32)
        o_ref[rows, :] = jnp.concatenate(outs, axis=1)
        return 0

    lax.fori_loop(0, n_chunks, chunk, 0)


def _hgrn(z, lb, ctx_len):
    b, s, _ = z.shape
    tb = ROW_BLOCK
    nt, nctx = s // tb, ctx_len // tb
    stack, masks = _hg_tables()

    def tblk(d, t):
        bwd = jnp.where(t < nctx, nctx - 1 - t, nt - 1 - (t - nctx))
        return jnp.where(d == 0, t, bwd)

    return pl.pallas_call(
        functools.partial(_hg_kernel, n_chunks=tb // HG_CHUNK),
        grid=(2, b, nt),
        in_specs=[
            pl.BlockSpec((None, tb, HG_WIDTH), lambda d, bi, t: (bi, tblk(d, t), 0)),
            pl.BlockSpec((None, tb, HG_WIDTH), lambda d, bi, t: (bi, tblk(d, t), 1 + d)),
            pl.BlockSpec((None, tb, HG_WIDTH), lambda d, bi, t: (bi, tblk(d, t), 3)),
            pl.BlockSpec((1, HG_WIDTH), lambda d, bi, t: (0, 0)),
            pl.BlockSpec((None, _HG_STACK_ROWS, 3 * HG_CHUNK), lambda d, bi, t: (d, 0, 0)),
            pl.BlockSpec((None, len(_HG_LEVELS) + 1, HG_CHUNK, HG_CHUNK),
                         lambda d, bi, t: (d, 0, 0, 0)),
        ],
        out_specs=pl.BlockSpec((None, None, tb, HG_WIDTH), lambda d, bi, t: (d, bi, tblk(d, t), 0)),
        out_shape=jax.ShapeDtypeStruct((2, b, s, HG_WIDTH), F32),
        scratch_shapes=[pltpu.VMEM((HG_HEADS, HG_HEAD_DIM, HG_HEAD_DIM), F32)],
        compiler_params=_cparams("parallel", "parallel", "arbitrary"),
    )(z, z, z, lb.reshape(1, HG_WIDTH), jnp.asarray(stack, BF16), jnp.asarray(masks))


def _rope_half(x, cos, sin_signed, half):
    width = x.shape[-1]
    if 2 * half == width:
        sw = pltpu.roll(x, half, 1)
    else:
        lane = lax.broadcasted_iota(jnp.int32, x.shape, 1)
        sw = jnp.where(lane % (2 * half) < half, pltpu.roll(x, width - half, 1), pltpu.roll(x, half, 1))
    return x * cos + sw * sin_signed


def _sw_kernel(q_ref, kc_ref, vc_ref, kp_ref, k0_ref, kn_ref, vp_ref, v0_ref, vn_ref,
               cq_ref, sq_ref, cp_ref, sp_ref, cn_ref, sn_ref, sink_ref, o_ref, *, ctx_len, seq):
    jb = pl.program_id(1)
    blk = ATT_BLOCK
    half = SW_HEAD_DIM // 2
    cq, sq = cq_ref[...], sq_ref[...]
    reps = q_ref.shape[-1] // LANES
    q = _rope_half(q_ref[0], jnp.concatenate([cq] * reps, axis=1),
                   jnp.concatenate([sq] * reps, axis=1), half)
    kw = jnp.concatenate([kp_ref[0], k0_ref[0], kn_ref[0]], axis=0)
    cw = jnp.concatenate([cp_ref[...], cq, cn_ref[...]], axis=0)
    sw = jnp.concatenate([sp_ref[...], sq, sn_ref[...]], axis=0)
    kw = _rope_half(kw, cw, sw, half)
    k_all = jnp.concatenate([kc_ref[0], kw], axis=0).astype(BF16)
    v_all = jnp.concatenate([vc_ref[0], vp_ref[0], v0_ref[0], vn_ref[0]], axis=0).astype(BF16)
    nk = ctx_len + 3 * blk
    col = lax.broadcasted_iota(jnp.int32, (blk, nk), 1)
    trow = jb * blk + lax.broadcasted_iota(jnp.int32, (blk, nk), 0)
    srow = (jb - 1) * blk + (col - ctx_len)
    band = ((trow >= ctx_len) & (srow >= ctx_len) & (srow < seq) & (jnp.abs(trow - srow) <= WINDOW))
    mask = (col < ctx_len) | band
    scale = SW_HEAD_DIM ** -0.5
    group = SW_Q_HEADS // SW_KV_HEADS
    outs = []
    for hq in range(SW_Q_HEADS):
        g = hq // group
        qh = q[:, hq * SW_HEAD_DIM:(hq + 1) * SW_HEAD_DIM].astype(BF16)
        kh = k_all[:, g * SW_HEAD_DIM:(g + 1) * SW_HEAD_DIM]
        vh = v_all[:, g * SW_HEAD_DIM:(g + 1) * SW_HEAD_DIM]
        s = lax.dot_general(qh, kh, _NT, preferred_element_type=F32) * scale
        s = jnp.where(mask, s, NEG_INF)
        sk = sink_ref[hq]
        m = jnp.maximum(jnp.max(s, axis=-1, keepdims=True), sk)
        e = jnp.exp(s - m)
        p = e / (jnp.sum(e, axis=-1, keepdims=True) + jnp.exp(sk - m))
        outs.append(jnp.dot(p.astype(BF16), vh, preferred_element_type=F32))
    o_ref[0] = jnp.concatenate(outs, axis=1)


def _sw_attention(z, cos, sin, sink, ctx_len):
    b, s, _ = z.shape
    blk = ATT_BLOCK
    nb = s // blk
    qw = SW_Q_HEADS * SW_HEAD_DIM
    qcol, kcol, vcol = 5 * HG_WIDTH // qw, 5 * HG_WIDTH // LANES + 4, 5 * HG_WIDTH // LANES + 5
    prev = lambda j: jnp.maximum(j - 1, 0)
    nxt = lambda j: jnp.minimum(j + 1, nb - 1)
    kv = lambda col, f: pl.BlockSpec((1, blk, LANES), lambda bi, j: (bi, f(j), col))
    tab = lambda f: pl.BlockSpec((blk, LANES), lambda bi, j: (f(j), 0))
    same = lambda j: j
    return pl.pallas_call(
        functools.partial(_sw_kernel, ctx_len=ctx_len, seq=s),
        grid=(b, nb),
        in_specs=[
            pl.BlockSpec((1, blk, qw), lambda bi, j: (bi, j, qcol)),
            pl.BlockSpec((1, ctx_len, LANES), lambda bi, j: (bi, 0, kcol)),
            pl.BlockSpec((1, ctx_len, LANES), lambda bi, j: (bi, 0, vcol)),
            kv(kcol, prev), kv(kcol, same), kv(kcol, nxt),
            kv(vcol, prev), kv(vcol, same), kv(vcol, nxt),
            tab(same), tab(same), tab(prev), tab(prev), tab(nxt), tab(nxt),
            pl.BlockSpec(memory_space=pltpu.SMEM),
        ],
        out_specs=pl.BlockSpec((1, blk, qw), lambda bi, j: (bi, j, 0)),
        out_shape=jax.ShapeDtypeStruct((b, s, qw), F32),
        compiler_params=_cparams("parallel", "parallel"),
    )(z, z, z, z, z, z, z, z, z, cos, sin, cos, sin, cos, sin, sink)


def _ab_out_kernel(of_ref, ob_ref, g_ref, sw_ref, h_ref, m_ref, ng_ref, w_ref, o_ref):
    o = of_ref[...] + ob_ref[...]
    ng = ng_ref[...]
    parts = []
    for hh in range(HG_HEADS):
        oh = o[:, hh * HG_HEAD_DIM:(hh + 1) * HG_HEAD_DIM]
        ms = jnp.mean(oh * oh, axis=-1, keepdims=True)
        parts.append(oh * lax.rsqrt(ms + EPS) * ng)
    hg = jnp.concatenate(parts, axis=1) * _silu(g_ref[0])
    mix = jnp.concatenate([hg, sw_ref[0]], axis=1).astype(BF16)
    y = jnp.dot(mix, w_ref[...], preferred_element_type=F32)
    o_ref[0] = h_ref[0] + m_ref[0, 0][2:3] * y


def _ab_out(o, z, sw, h, mod, ng, w, ctx_blocks):
    b, s, d = h.shape
    tm = ROW_BLOCK
    sel = lambda i: jnp.minimum(i // ctx_blocks, 1)
    return pl.pallas_call(
        _ab_out_kernel,
        grid=(b, s // tm),
        in_specs=[
            pl.BlockSpec((None, None, tm, HG_WIDTH), lambda bi, i: (0, bi, i, 0)),
            pl.BlockSpec((None, None, tm, HG_WIDTH), lambda bi, i: (1, bi, i, 0)),
            pl.BlockSpec((1, tm, HG_WIDTH), lambda bi, i: (bi, i, 4)),
            pl.BlockSpec((1, tm, sw.shape[-1]), lambda bi, i: (bi, i, 0)),
            pl.BlockSpec((1, tm, d), lambda bi, i: (bi, i, 0)),
            pl.BlockSpec((1, 1, 8, d), lambda bi, i: (bi, sel(i), 0, 0)),
            pl.BlockSpec((1, HG_HEAD_DIM), lambda bi, i: (0, 0)),
            pl.BlockSpec(w.shape, lambda bi, i: (0, 0)),
        ],
        out_specs=pl.BlockSpec((1, tm, d), lambda bi, i: (bi, i, 0)),
        out_shape=jax.ShapeDtypeStruct((b, s, d), F32),
        compiler_params=_cparams("parallel", "parallel"),
    )(o, o, z, sw, h, mod, ng.reshape(1, HG_HEAD_DIM), w)


def _ga_prep_kernel(z_ref, cos_ref, sin_ref, qg_ref, kg_ref, q_ref, k_ref, v_ref):
    z = z_ref[0]
    cos, sin = cos_ref[...], sin_ref[...]
    hd = GA_HEAD_DIM

    def norm_rope(x, g):
        ms = jnp.mean(x * x, axis=-1, keepdims=True)
        return _rope_half(x * lax.rsqrt(ms + EPS) * g, cos, sin, hd // 2)

    qg, kg = qg_ref[...], kg_ref[...]
    qs = [norm_rope(z[:, i * hd:(i + 1) * hd], qg) * (hd ** -0.5) for i in range(GA_Q_HEADS)]
    q_ref[0] = jnp.concatenate(qs, axis=1).astype(BF16)
    k0 = GA_Q_HEADS * hd
    ks = [norm_rope(z[:, k0 + i * hd:k0 + (i + 1) * hd], kg) for i in range(GA_KV_HEADS)]
    k_ref[0] = jnp.concatenate(ks, axis=1).astype(BF16)
    v_ref[0] = z[:, k0 + GA_KV_HEADS * hd:].astype(BF16)


def _ga_prep(z, cos, sin, qg, kg):
    b, s, n = z.shape
    tm = ROW_BLOCK
    qw, kw = GA_Q_HEADS * GA_HEAD_DIM, GA_KV_HEADS * GA_HEAD_DIM
    row = lambda width: pl.BlockSpec((1, tm, width), lambda bi, i: (bi, i, 0))
    return pl.pallas_call(
        _ga_prep_kernel,
        grid=(b, s // tm),
        in_specs=[
            row(n),
            pl.BlockSpec((tm, GA_HEAD_DIM), lambda bi, i: (i, 0)),
            pl.BlockSpec((tm, GA_HEAD_DIM), lambda bi, i: (i, 0)),
            pl.BlockSpec((1, GA_HEAD_DIM), lambda bi, i: (0, 0)),
            pl.BlockSpec((1, GA_HEAD_DIM), lambda bi, i: (0, 0)),
        ],
        out_specs=[row(qw), row(kw), row(kw)],
        out_shape=[jax.ShapeDtypeStruct((b, s, qw), BF16),
                   jax.ShapeDtypeStruct((b, s, kw), BF16),
                   jax.ShapeDtypeStruct((b, s, kw), BF16)],
        compiler_params=_cparams("parallel", "parallel"),
    )(z, cos, sin, qg.reshape(1, -1), kg.reshape(1, -1))


_GA_TQ = 128
_GA_CHUNKS = 2


def _flash_kernel(q_ref, k_ref, v_ref, o_ref, s_ref):
    hd = GA_HEAD_DIM
    group = GA_Q_HEADS // GA_KV_HEADS
    q = q_ref[0]
    tq = q.shape[0]
    qs = jnp.concatenate([q[:, i * hd:(i + 1) * hd] for i in range(group)], axis=0)
    nk = k_ref.shape[1]
    kc = nk // _GA_CHUNKS
    for c in range(_GA_CHUNKS):
        s_ref[:, c * kc:(c + 1) * kc] = lax.dot_general(qs, k_ref[0, c * kc:(c + 1) * kc, :], _NT,
                                                        preferred_element_type=F32)
    tile = lambda rows, j: s_ref[rows, j * LANES:(j + 1) * LANES]
    outs = []
    for g in range(group):
        rows = slice(g * tq, (g + 1) * tq)
        m = tile(rows, 0)
        for j in range(1, nk // LANES):
            m = jnp.maximum(m, tile(rows, j))
        m = jnp.broadcast_to(jnp.max(m, axis=-1, keepdims=True), m.shape)
        lsum = jnp.zeros_like(m)
        ps = []
        for j in range(nk // LANES):
            pj = jnp.exp(tile(rows, j) - m)
            lsum = lsum + pj
            ps.append(pj.astype(BF16))
        o = jnp.dot(jnp.concatenate(ps, axis=1), v_ref[0], preferred_element_type=F32)
        outs.append(o / jnp.sum(lsum, axis=-1, keepdims=True))
    o_ref[0] = jnp.concatenate(outs, axis=1).astype(o_ref.dtype)


def _ga_attention(q, k, v, ctx_len):
    b, s, qw = q.shape
    l = s - ctx_len
    tq = _GA_TQ
    group = GA_Q_HEADS // GA_KV_HEADS
    gw = group * GA_HEAD_DIM
    off = ctx_len // tq
    return pl.pallas_call(
        _flash_kernel,
        grid=(b, GA_KV_HEADS, l // tq),
        in_specs=[
            pl.BlockSpec((1, tq, gw), lambda bi, g, i: (bi, off + i, g)),
            pl.BlockSpec((1, s, GA_HEAD_DIM), lambda bi, g, i: (bi, 0, g)),
            pl.BlockSpec((1, s, GA_HEAD_DIM), lambda bi, g, i: (bi, 0, g)),
        ],
        out_specs=pl.BlockSpec((1, tq, gw), lambda bi, g, i: (bi, i, g)),
        out_shape=jax.ShapeDtypeStruct((b, l, qw), BF16),
        scratch_shapes=[pltpu.VMEM((group * tq, s), F32)],
        compiler_params=_cparams("parallel", "parallel", "parallel"),
    )(q, k, v)


def _out_res_kernel(a_ref, h_ref, m_ref, w_ref, o_ref):
    y = jnp.dot(a_ref[0], w_ref[...], preferred_element_type=F32)
    o_ref[0] = h_ref[0] + m_ref[0, 0][2:3] * y


def _out_res(a, h, mod, w, ctx_len):
    b, l, n = a.shape
    d = h.shape[-1]
    tm = ROW_BLOCK
    off = ctx_len // tm
    return pl.pallas_call(
        _out_res_kernel,
        grid=(b, l // tm),
        in_specs=[
            pl.BlockSpec((1, tm, n), lambda bi, i: (bi, i, 0)),
            pl.BlockSpec((1, tm, d), lambda bi, i: (bi, off + i, 0)),
            pl.BlockSpec((1, 1, 8, d), lambda bi, i: (bi, 1, 0, 0)),
            pl.BlockSpec(w.shape, lambda bi, i: (0, 0)),
        ],
        out_specs=pl.BlockSpec((1, tm, d), lambda bi, i: (bi, i, 0)),
        out_shape=jax.ShapeDtypeStruct((b, l, d), F32),
        compiler_params=_cparams("parallel", "parallel"),
    )(a, h, mod, w)


_NO_RANK = 1 << 20


def _topk_rows(s, rank=None, payload=None):
    if rank is None:
        rank = lax.broadcasted_iota(jnp.int32, s.shape, 0)
    vals, sel = [], []
    for _ in range(PEER_TOPK):
        m = jnp.max(s, axis=0, keepdims=True)
        i = jnp.min(jnp.where(s == m, rank, _NO_RANK), axis=0, keepdims=True)
        hit = rank == i
        vals.append(m)
        sel.append(i if payload is None else jnp.max(jnp.where(hit, payload, -1), axis=0, keepdims=True))
        s = jnp.where(hit, -jnp.inf, s)
    return jnp.concatenate(vals, axis=0), jnp.concatenate(sel, axis=0)


def _cand_slabs():
    k = PEER_TOPK
    slabs = [('b', 0, 0), ('b', 0, 8)] + [('b', a, 0) for a in range(1, 8)] + [('a', 8, 0)]
    rank = []
    for kind, a, b in slabs:
        for j in range(SUBLANES):
            aa, bb = (a, b + j) if kind == 'b' else (a + j, b)
            rank.append(aa * k + bb if (aa + 1) * (bb + 1) <= k else _NO_RANK)
    return slabs, np.asarray(rank, np.int32)


def _route_head(q_ref, keys_ref, rank, h):
    slabs, _ = _cand_slabs()
    tops = []
    for p in range(2):
        c0 = (h * 2 + p) * PEER_HALF
        st = lax.dot_general(keys_ref[h, p], q_ref[0, :, c0:c0 + PEER_HALF], _NT,
                             precision=HIGHEST, preferred_element_type=F32)
        tops.append(_topk_rows(st))
    (s1, i1), (s2, i2) = tops
    pick = lambda x, y, kind, a, b: ((x[a:a + 1], y[b:b + SUBLANES]) if kind == 'b'
                                     else (x[a:a + SUBLANES], y[b:b + 1]))
    cand = jnp.concatenate([sum(pick(s1, s2, *sl)) for sl in slabs], axis=0)
    cand = jnp.where(rank < _NO_RANK, cand, -jnp.inf)
    eid = jnp.concatenate([(lambda x, y: x * PEER_N_KEYS + y)(*pick(i1, i2, *sl)) for sl in slabs],
                          axis=0)
    sc, e = _topk_rows(cand, rank, eid)
    ex = jnp.exp(sc - jnp.max(sc, axis=0, keepdims=True))
    return e, ex / jnp.sum(ex, axis=0, keepdims=True)


_EX_TOK = 8
_EX_STEP_TOK = LANES
_EX_HALVES = _EX_STEP_TOK // _EX_TOK
_EX_BUFS = 4
_EX_AHEAD = 2
_EX_NP = _EX_TOK * PEER_PAIRS
_ROW_TILES = 8
_ROW_PITCH = 12
_EX_GROUPS = 2 * _ROW_TILES


def _peer_kernel(qc_ref, qn_ref, keys_ref, rank_ref, f_ref, h_ref, m_ref, fg_ref, ex_ref, uv_ref, o_ref,
                 *scratch, n_steps, final):
    bufs = scratch[:_EX_BUFS]
    sem, ids_ring, gate_ring, ids_stage, ids_sem = scratch[_EX_BUFS:]
    step = pl.program_id(0) * pl.num_programs(1) + pl.program_id(1)
    slot = step % 2
    per_group = _EX_NP // _EX_GROUPS
    rank = rank_ref[...]

    def start_copy(e, dst, p, sem_i, priority):
        pltpu.make_async_copy(uv_ref.at[e], dst.at[pl.ds(p * _ROW_PITCH, _ROW_TILES), :],
                              sem.at[sem_i]).start(priority=priority)

    def wait_all(dst, sem_i):
        rows = pl.ds(0, _EX_NP * _ROW_TILES)
        pltpu.make_async_copy(dst.at[rows, :], dst.at[rows, :], sem.at[sem_i]).wait()

    def half(cur, half_i, ids_slot, tok0, other, other_sem, extra):
        rows = slice(half_i * _EX_TOK, (half_i + 1) * _EX_TOK)
        f = f_ref[0, rows, :]

        def issue(g):
            for u in range(per_group):
                p = g * per_group + u
                start_copy(ids_ring[ids_slot, tok0 + p // PEER_PAIRS, p % PEER_PAIRS], other, p,
                           other_sem, u % 2)

        words = lambda r: pltpu.bitcast(cur[pl.ds(r, _EX_NP, stride=_ROW_PITCH), :], BF16)
        a = jnp.zeros((_EX_TOK, _EX_NP), F32)
        for r in range(_ROW_TILES):
            issue(r)
            u = pltpu.bitcast(lax.shift_left(cur[pl.ds(r, _EX_NP, stride=_ROW_PITCH), :], 16), F32)
            a = a + lax.dot_general(f[:, r * LANES:(r + 1) * LANES], u.astype(BF16), _NT,
                                    preferred_element_type=F32)
        extra()
        act = 0.5 * a * (1.0 + lax.erf(a * (2.0 ** -0.5)))
        gate = jnp.concatenate([gate_ring[slot, rows, :]] * _EX_TOK, axis=1)
        own = (lax.broadcasted_iota(jnp.int32, a.shape, 1) // PEER_PAIRS
               == lax.broadcasted_iota(jnp.int32, a.shape, 0))
        ga = jnp.where(own, gate * act, 0.0).astype(BF16)
        coef = jnp.concatenate(
            [jnp.dot(ga[:, t * PEER_PAIRS:(t + 1) * PEER_PAIRS], ex_ref[...], preferred_element_type=F32)
             for t in range(_EX_TOK)], axis=1).astype(BF16)
        ys = []
        for r in range(_ROW_TILES):
            issue(_ROW_TILES + r)
            ys.append(jnp.dot(coef, words(r), preferred_element_type=F32))
        hn = h_ref[0, rows, :] + m_ref[0, 0][5:6] * jnp.concatenate(ys, axis=1)
        if final:
            ms = jnp.mean(hn * hn, axis=-1, keepdims=True)
            hn = hn * lax.rsqrt(ms + EPS) * fg_ref[...]
        o_ref[0, rows, :] = hn

    def publish(routed, to_slot):
        ids_stage[...] = jnp.concatenate([e for e, _ in routed], axis=0).astype(F32).T.astype(jnp.int32)
        gate_ring[to_slot] = jnp.concatenate([g for _, g in routed], axis=0).T
        cp = pltpu.make_async_copy(ids_stage, ids_ring.at[to_slot], ids_sem.at[0])
        cp.start()
        cp.wait()

    @pl.when(step == 0)
    def _():
        publish([_route_head(qc_ref, keys_ref, rank, h) for h in range(PEER_HEADS)], 0)
        for k in range(_EX_AHEAD):
            def body(p, _, k=k):
                start_copy(ids_ring[0, k * _EX_TOK + p // PEER_PAIRS, p % PEER_PAIRS], bufs[k], p, k, 0)
                return 0
            lax.fori_loop(0, _EX_NP, body, 0)

    routed = []
    for k in range(_EX_HALVES):
        wait_all(bufs[k % _EX_BUFS], k % _EX_BUFS)
        tgt = k + _EX_AHEAD
        ids_slot, tok0 = (slot, tgt * _EX_TOK) if tgt < _EX_HALVES else (1 - slot, (tgt - _EX_HALVES) * _EX_TOK)
        if k < PEER_HEADS:
            extra = lambda k=k: routed.append(_route_head(qn_ref, keys_ref, rank, k))
        elif k == PEER_HEADS:
            extra = lambda: publish(routed, 1 - slot)
        else:
            extra = lambda: None
        half(bufs[k % _EX_BUFS], k, ids_slot, tok0, bufs[tgt % _EX_BUFS], tgt % _EX_BUFS, extra)

    @pl.when(step == n_steps - 1)
    def _():
        for k in range(_EX_AHEAD):
            wait_all(bufs[k], k)


def _peer(q, keys, f, h, mod, uv, final_g, *, ctx_blocks, final):
    b, r, d = h.shape
    tok = _EX_STEP_TOK
    nb = r // tok
    n_steps = b * nb
    per_ctx = ctx_blocks * (ROW_BLOCK // tok)
    sel = (lambda i: jnp.minimum(i // per_ctx, 1)) if ctx_blocks else (lambda i: 1)
    blk = lambda width: pl.BlockSpec((1, tok, width), lambda bi, i: (bi, i, 0))

    def next_blk(bi, i):
        nxt = jnp.minimum(bi * nb + i + 1, n_steps - 1)
        return (nxt // nb, nxt % nb, 0)

    expand = np.zeros((PEER_PAIRS, 2 * PEER_PAIRS), np.float32)
    expand[np.arange(PEER_PAIRS), 2 * np.arange(PEER_PAIRS) + 1] = 1.0
    rank = np.tile(_cand_slabs()[1][:, None], (1, tok))
    buf = pltpu.VMEM((_EX_NP * _ROW_PITCH, LANES), jnp.int32)
    return pl.pallas_call(
        functools.partial(_peer_kernel, n_steps=n_steps, final=final),
        grid=(b, nb),
        in_specs=[
            blk(q.shape[-1]),
            pl.BlockSpec((1, tok, q.shape[-1]), next_blk),
            pl.BlockSpec(keys.shape, lambda bi, i: (0, 0, 0, 0)),
            pl.BlockSpec(rank.shape, lambda bi, i: (0, 0)),
            blk(d), blk(d),
            pl.BlockSpec((1, 1, 8, d), lambda bi, i: (bi, sel(i), 0, 0)),
            pl.BlockSpec((1, d), lambda bi, i: (0, 0)),
            pl.BlockSpec(expand.shape, lambda bi, i: (0, 0)),
            pl.BlockSpec(memory_space=pl.ANY),
        ],
        out_specs=blk(d),
        out_shape=jax.ShapeDtypeStruct((b, r, d), F32),
        scratch_shapes=[buf] * _EX_BUFS + [
            pltpu.SemaphoreType.DMA((_EX_BUFS,)),
            pltpu.SMEM((2, tok, PEER_PAIRS), jnp.int32),
            pltpu.VMEM((2, tok, PEER_PAIRS), F32),
            pltpu.VMEM((tok, PEER_PAIRS), jnp.int32),
            pltpu.SemaphoreType.DMA((1,)),
        ],
        compiler_params=_cparams("arbitrary", "arbitrary"),
    )(q, q, keys, jnp.asarray(rank), f, h, mod, final_g.reshape(1, d), jnp.asarray(expand, BF16), uv)


def _pack_experts(u, v):
    ub = lax.bitcast_convert_type(u.astype(BF16), jnp.uint16).astype(jnp.uint32)
    vb = lax.bitcast_convert_type(v.astype(BF16), jnp.uint16).astype(jnp.uint32)
    words = lax.bitcast_convert_type(ub | (vb << 16), jnp.int32)
    return words.reshape(u.shape[0], _ROW_TILES, LANES)


def _rope_tables(seq, ctx_len, head_dim):
    n_rows = seq // GRID_W
    row = jnp.repeat(jnp.arange(n_rows), GRID_W).astype(F32)
    col = jnp.tile(jnp.arange(GRID_W), n_rows).astype(F32)
    n_freq = head_dim // 4
    inv = ROPE_THETA ** (-jnp.arange(n_freq, dtype=F32) / n_freq)
    ang = jnp.concatenate([row[:, None] * inv, col[:, None] * inv], axis=-1)
    cos = jnp.concatenate([jnp.cos(ang), jnp.cos(ang)], axis=-1)
    sin = jnp.concatenate([-jnp.sin(ang), jnp.sin(ang)], axis=-1)
    reps = LANES // head_dim
    cos, sin = jnp.tile(cos, (1, reps)), jnp.tile(sin, (1, reps))
    cos = jnp.concatenate([jnp.ones((ctx_len, LANES), F32), cos], axis=0)
    sin = jnp.concatenate([jnp.zeros((ctx_len, LANES), F32), sin], axis=0)
    return cos, sin


def kernel(x, c, ctx, c_ctx, ada_w, ada_b, norm_mix_g, norm_ffn_g, ab_w_in, ab_w_out, hg_lb_logits,
           hg_norm_g, sw_sink, ga_w_in, ga_w_out, ga_q_norm_g, ga_k_norm_g, peer_w_q, peer_sub_keys,
           peer_u, peer_v, final_norm_g):
    b, l, d = x.shape
    lc = ctx.shape[1]
    assert lc % ROW_BLOCK == 0 and l % ROW_BLOCK == 0 and ada_w.shape[0] == 2
    ctx_blocks = lc // ROW_BLOCK
    h = jnp.concatenate([ctx, x], axis=1)

    n_mod = -(-(b + 1) // SUBLANES) * SUBLANES
    cc = jnp.zeros((n_mod, d), F32).at[:b].set(c).at[b].set(c_ctx)
    mod_all = _ada(cc, ada_w, ada_b)

    def mod_for(layer):
        lat = mod_all[layer, :b].reshape(b, 1, 6, d)
        cx = jnp.broadcast_to(mod_all[layer, b].reshape(1, 1, 6, d), (b, 1, 6, d))
        m = jnp.concatenate([cx, lat], axis=1)
        return jnp.pad(m, ((0, 0), (0, 0), (0, 2), (0, 0)))

    lbs = jnp.cumsum(jax.nn.softmax(hg_lb_logits.astype(F32), axis=0), axis=0)

    m0 = mod_for(0)
    z = _nmm(h, m0, norm_mix_g[0], ab_w_in[0].astype(BF16), row0=0, ctx_blocks=ctx_blocks)
    o = _hgrn(z, lbs[0], lc)
    cos_sw, sin_sw = _rope_tables(l, lc, SW_HEAD_DIM)
    sw = _sw_attention(z, cos_sw, sin_sw, sw_sink[0], lc)
    h = _ab_out(o, z, sw, h, m0, hg_norm_g[0], ab_w_out[0].astype(BF16), ctx_blocks)
    q, f = _nmm(h, m0, norm_ffn_g[0], peer_w_q[0].astype(BF16), row0=3, ctx_blocks=ctx_blocks,
                want_f=True)
    h = _peer(q, peer_sub_keys[0], f, h, m0, _pack_experts(peer_u[0], peer_v[0]), final_norm_g,
              ctx_blocks=ctx_blocks, final=False)

    m1 = mod_for(1)
    z = _nmm(h, m1, norm_mix_g[1], ga_w_in[0].astype(BF16), row0=0, ctx_blocks=ctx_blocks)
    cos_ga, sin_ga = _rope_tables(l, lc, GA_HEAD_DIM)
    qg, kg, vg = _ga_prep(z, cos_ga, sin_ga, ga_q_norm_g[0], ga_k_norm_g[0])
    att = _ga_attention(qg, kg, vg, lc)
    hl = _out_res(att, h, m1, ga_w_out[0].astype(BF16), lc)
    q, f = _nmm(hl, m1, norm_ffn_g[1], peer_w_q[1].astype(BF16), row0=3, ctx_blocks=0, want_f=True)
    return _peer(q, peer_sub_keys[1], f, hl, m1, _pack_experts(peer_u[1], peer_v[1]), final_norm_g,
                 ctx_blocks=0, final=True)
```

```python
import functools

import numpy as np
import jax
import jax.numpy as jnp
from jax import lax
from jax.experimental import pallas as pl
from jax.experimental.pallas import tpu as pltpu

F32 = jnp.float32
BF16 = jnp.bfloat16
HIGHEST = lax.Precision.HIGHEST

EPS = 1e-6
NEG_INF = -1e30
ROPE_THETA = 10000.0
GRID_W = 64

HG_HEAD_DIM = 128
HG_HEADS = 4
HG_WIDTH = HG_HEADS * HG_HEAD_DIM
HG_CHUNK = 64
SW_HEAD_DIM = 64
SW_Q_HEADS = 8
SW_KV_HEADS = 2
WINDOW = 128
ATT_BLOCK = 128
GA_HEAD_DIM = 128
GA_Q_HEADS = 8
GA_KV_HEADS = 2
PEER_HEADS = 8
PEER_N_KEYS = 128
PEER_HALF = 128
PEER_TOPK = 16
PEER_PAIRS = PEER_HEADS * PEER_TOPK

LANES = 128
SUBLANES = 8
ROW_BLOCK = 256
VMEM_LIMIT = 56 * 1024 * 1024

_NT = (((1,), (1,)), ((), ()))
_TN = (((0,), (0,)), ((), ()))


def _cparams(*sem):
    return pltpu.CompilerParams(dimension_semantics=sem, vmem_limit_bytes=VMEM_LIMIT)


def _sigmoid(x):
    return 1.0 / (1.0 + jnp.exp(-x))


def _silu(x):
    return x * _sigmoid(x)


def _ada_kernel(c_ref, w_ref, b_ref, o_ref):
    a = _silu(c_ref[...])
    o_ref[0] = jnp.dot(a, w_ref[0], precision=HIGHEST, preferred_element_type=F32) + b_ref[0]


def _ada(cc, ada_w, ada_b):
    depth, d, n = ada_w.shape
    rows = cc.shape[0]
    tn = 1536
    return pl.pallas_call(
        _ada_kernel,
        grid=(depth, n // tn),
        in_specs=[
            pl.BlockSpec((rows, d), lambda l, j: (0, 0)),
            pl.BlockSpec((1, d, tn), lambda l, j: (l, 0, j)),
            pl.BlockSpec((1, 1, tn), lambda l, j: (l, 0, j)),
        ],
        out_specs=pl.BlockSpec((1, rows, tn), lambda l, j: (l, 0, j)),
        out_shape=jax.ShapeDtypeStruct((depth, rows, n), F32),
        compiler_params=_cparams("parallel", "parallel"),
    )(cc, ada_w, ada_b.reshape(depth, 1, n))


def _nmm_kernel(h_ref, m_ref, g_ref, w_ref, o_ref, *f_ref, row0):
    x = h_ref[0]
    ms = jnp.mean(x * x, axis=-1, keepdims=True)
    y = x * lax.rsqrt(ms + EPS) * g_ref[...]
    m = m_ref[0, 0]
    a = (y * (1.0 + m[row0 + 1:row0 + 2]) + m[row0:row0 + 1]).astype(BF16)
    if f_ref:
        f_ref[0][0] = a
    o_ref[0] = jnp.dot(a, w_ref[...], preferred_element_type=F32)


def _nmm(h, mod, g, w, *, row0, ctx_blocks, want_f=False):
    b, r, d = h.shape
    n = w.shape[1]
    tm = ROW_BLOCK
    sel = (lambda i: jnp.minimum(i // ctx_blocks, 1)) if ctx_blocks else (lambda i: 1)
    out_shape = [jax.ShapeDtypeStruct((b, r, n), F32)]
    out_specs = [pl.BlockSpec((1, tm, n), lambda bi, i: (bi, i, 0))]
    if want_f:
        out_shape.append(jax.ShapeDtypeStruct((b, r, d), BF16))
        out_specs.append(pl.BlockSpec((1, tm, d), lambda bi, i: (bi, i, 0)))
    res = pl.pallas_call(
        functools.partial(_nmm_kernel, row0=row0),
        grid=(b, r // tm),
        in_specs=[
            pl.BlockSpec((1, tm, d), lambda bi, i: (bi, i, 0)),
            pl.BlockSpec((1, 1, 8, d), lambda bi, i: (bi, sel(i), 0, 0)),
            pl.BlockSpec((1, d), lambda bi, i: (0, 0)),
            pl.BlockSpec((d, n), lambda bi, i: (0, 0)),
        ],
        out_specs=out_specs,
        out_shape=out_shape,
        compiler_params=_cparams("parallel", "parallel"),
    )(h, mod, g.reshape(1, d), w)
    return res if want_f else res[0]


_HG_LEVELS = (32, 16, 8, 4, 2, 1)
_HG_STACK_ROWS = HG_CHUNK * (2 + 2 * len(_HG_LEVELS)) + SUBLANES


def _hg_tables():
    c = HG_CHUNK
    stacks, masks = [], []
    for rev in (0, 1):
        pos = np.arange(c) if rev == 0 else c - 1 - np.arange(c)
        tt, ss = pos[:, None], pos[None, :]
        rows = [(ss <= tt)]
        qrows, krows, lmask = [], [], []
        for m in _HG_LEVELS:
            same = (tt // (2 * m)) == (ss // (2 * m))
            pt, ps = tt % (2 * m), ss % (2 * m)
            qrows.append(same & (pt >= m) & (ps >= m) & (ps <= pt))
            krows.append(same & (pt < m) & (ps > pt) & (ps <= m - 1))
            lmask.append(same & (pt >= m) & (ps < m))
        lmask.append(tt == ss)
        rows += qrows + krows + [(ss > tt), np.ones((SUBLANES, c), bool)]
        stacks.append(np.tile(np.concatenate(rows, axis=0).astype(np.float32), (1, 3)))
        masks.append(np.stack(lmask).astype(np.float32))
    return np.stack(stacks), np.stack(masks)


def _hg_kernel(q_ref, f_ref, i_ref, lb_ref, w_ref, ml_ref, o_ref, st_ref, *, n_chunks):
    d = pl.program_id(0)
    t = pl.program_id(2)
    c = HG_CHUNK
    nl = len(_HG_LEVELS)

    @pl.when(t == 0)
    def _():
        st_ref[...] = jnp.zeros_like(st_ref)

    lb = lb_ref[...]
    w = w_ref[...]

    def chunk(ci, _):
        cc = jnp.where(d == 0, ci, n_chunks - 1 - ci)
        rows = pl.ds(pl.multiple_of(cc * c, c), c)
        x = f_ref[rows, :]
        k = (1.0 - lb) * _sigmoid(-x)
        lf = jnp.log(1.0 - k)
        q = _silu(q_ref[rows, :])
        v = i_ref[rows, :]
        hi = lf.astype(BF16)
        r1 = lf - hi.astype(F32)
        mid = r1.astype(BF16)
        lo = (r1 - mid.astype(F32)).astype(BF16)
        e = jnp.exp(jnp.dot(w, jnp.concatenate([hi, mid, lo], axis=0), preferred_element_type=F32))
        qe = q * e[0:c]
        kend = k * e[c * (1 + 2 * nl):c * (2 + 2 * nl)]
        tot = e[c * (2 + 2 * nl):c * (2 + 2 * nl) + 1]
        scaled = [(q * e[c * (1 + l):c * (2 + l)], k * e[c * (1 + nl + l):c * (2 + nl + l)])
                  for l in range(nl)]
        outs = []
        for hh in range(HG_HEADS):
            hs = slice(hh * HG_HEAD_DIM, (hh + 1) * HG_HEAD_DIM)
            st = st_ref[hh]
            o = lax.dot_general(qe[:, hs], st, _NT, preferred_element_type=F32)
            a = ml_ref[nl] * lax.dot_general(q[:, hs], k[:, hs], _NT, preferred_element_type=F32)
            for l in range(nl):
                a = a + ml_ref[l] * lax.dot_general(scaled[l][0][:, hs], scaled[l][1][:, hs], _NT,
                                                    preferred_element_type=F32)
            outs.append(o + jnp.dot(a, v[:, hs], preferred_element_type=F32))
            st_ref[hh] = st * tot[:, hs] + lax.dot_general(v[:, hs], kend[:, hs], _TN,
                                                           preferred_element_type=F32)
        o_ref[rows, :] = jnp.concatenate(outs, axis=1)
        return 0

    lax.fori_loop(0, n_chunks, chunk, 0)


def _hgrn(z, lb, ctx_len):
    b, s, _ = z.shape
    tb = ROW_BLOCK
    nt, nctx = s // tb, ctx_len // tb
    stack, masks = _hg_tables()

    def tblk(d, t):
        bwd = jnp.where(t < nctx, nctx - 1 - t, nt - 1 - (t - nctx))
        return jnp.where(d == 0, t, bwd)

    return pl.pallas_call(
        functools.partial(_hg_kernel, n_chunks=tb // HG_CHUNK),
        grid=(2, b, nt),
        in_specs=[
            pl.BlockSpec((None, tb, HG_WIDTH), lambda d, bi, t: (bi, tblk(d, t), 0)),
            pl.BlockSpec((None, tb, HG_WIDTH), lambda d, bi, t: (bi, tblk(d, t), 1 + d)),
            pl.BlockSpec((None, tb, HG_WIDTH), lambda d, bi, t: (bi, tblk(d, t), 3)),
            pl.BlockSpec((1, HG_WIDTH), lambda d, bi, t: (0, 0)),
            pl.BlockSpec((None, _HG_STACK_ROWS, 3 * HG_CHUNK), lambda d, bi, t: (d, 0, 0)),
            pl.BlockSpec((None, len(_HG_LEVELS) + 1, HG_CHUNK, HG_CHUNK),
                         lambda d, bi, t: (d, 0, 0, 0)),
        ],
        out_specs=pl.BlockSpec((None, None, tb, HG_WIDTH), lambda d, bi, t: (d, bi, tblk(d, t), 0)),
        out_shape=jax.ShapeDtypeStruct((2, b, s, HG_WIDTH), F32),
        scratch_shapes=[pltpu.VMEM((HG_HEADS, HG_HEAD_DIM, HG_HEAD_DIM), F32)],
        compiler_params=_cparams("parallel", "parallel", "arbitrary"),
    )(z, z, z, lb.reshape(1, HG_WIDTH), jnp.asarray(stack, BF16), jnp.asarray(masks))


def _rope_half(x, cos, sin_signed, half):
    width = x.shape[-1]
    if 2 * half == width:
        sw = pltpu.roll(x, half, 1)
    else:
        lane = lax.broadcasted_iota(jnp.int32, x.shape, 1)
        sw = jnp.where(lane % (2 * half) < half, pltpu.roll(x, width - half, 1), pltpu.roll(x, half, 1))
    return x * cos + sw * sin_signed


def _sw_kernel(q_ref, kc_ref, vc_ref, kp_ref, k0_ref, kn_ref, vp_ref, v0_ref, vn_ref,
               cq_ref, sq_ref, cp_ref, sp_ref, cn_ref, sn_ref, sink_ref, o_ref, *, ctx_len, seq):
    jb = pl.program_id(1)
    blk = ATT_BLOCK
    half = SW_HEAD_DIM // 2
    cq, sq = cq_ref[...], sq_ref[...]
    reps = q_ref.shape[-1] // LANES
    q = _rope_half(q_ref[0], jnp.concatenate([cq] * reps, axis=1),
                   jnp.concatenate([sq] * reps, axis=1), half)
    kw = jnp.concatenate([kp_ref[0], k0_ref[0], kn_ref[0]], axis=0)
    cw = jnp.concatenate([cp_ref[...], cq, cn_ref[...]], axis=0)
    sw = jnp.concatenate([sp_ref[...], sq, sn_ref[...]], axis=0)
    kw = _rope_half(kw, cw, sw, half)
    k_all = jnp.concatenate([kc_ref[0], kw], axis=0).astype(BF16)
    v_all = jnp.concatenate([vc_ref[0], vp_ref[0], v0_ref[0], vn_ref[0]], axis=0).astype(BF16)
    nk = ctx_len + 3 * blk
    col = lax.broadcasted_iota(jnp.int32, (blk, nk), 1)
    trow = jb * blk + lax.broadcasted_iota(jnp.int32, (blk, nk), 0)
    srow = (jb - 1) * blk + (col - ctx_len)
    band = ((trow >= ctx_len) & (srow >= ctx_len) & (srow < seq) & (jnp.abs(trow - srow) <= WINDOW))
    mask = (col < ctx_len) | band
    scale = SW_HEAD_DIM ** -0.5
    group = SW_Q_HEADS // SW_KV_HEADS
    outs = []
    for hq in range(SW_Q_HEADS):
        g = hq // group
        qh = q[:, hq * SW_HEAD_DIM:(hq + 1) * SW_HEAD_DIM].astype(BF16)
        kh = k_all[:, g * SW_HEAD_DIM:(g + 1) * SW_HEAD_DIM]
        vh = v_all[:, g * SW_HEAD_DIM:(g + 1) * SW_HEAD_DIM]
        s = lax.dot_general(qh, kh, _NT, preferred_element_type=F32) * scale
        s = jnp.where(mask, s, NEG_INF)
        sk = sink_ref[hq]
        m = jnp.maximum(jnp.max(s, axis=-1, keepdims=True), sk)
        e = jnp.exp(s - m)
        p = e / (jnp.sum(e, axis=-1, keepdims=True) + jnp.exp(sk - m))
        outs.append(jnp.dot(p.astype(BF16), vh, preferred_element_type=F32))
    o_ref[0] = jnp.concatenate(outs, axis=1)


def _sw_attention(z, cos, sin, sink, ctx_len):
    b, s, _ = z.shape
    blk = ATT_BLOCK
    nb = s // blk
    qw = SW_Q_HEADS * SW_HEAD_DIM
    qcol, kcol, vcol = 5 * HG_WIDTH // qw, 5 * HG_WIDTH // LANES + 4, 5 * HG_WIDTH // LANES + 5
    prev = lambda j: jnp.maximum(j - 1, 0)
    nxt = lambda j: jnp.minimum(j + 1, nb - 1)
    kv = lambda col, f: pl.BlockSpec((1, blk, LANES), lambda bi, j: (bi, f(j), col))
    tab = lambda f: pl.BlockSpec((blk, LANES), lambda bi, j: (f(j), 0))
    same = lambda j: j
    return pl.pallas_call(
        functools.partial(_sw_kernel, ctx_len=ctx_len, seq=s),
        grid=(b, nb),
        in_specs=[
            pl.BlockSpec((1, blk, qw), lambda bi, j: (bi, j, qcol)),
            pl.BlockSpec((1, ctx_len, LANES), lambda bi, j: (bi, 0, kcol)),
            pl.BlockSpec((1, ctx_len, LANES), lambda bi, j: (bi, 0, vcol)),
            kv(kcol, prev), kv(kcol, same), kv(kcol, nxt),
            kv(vcol, prev), kv(vcol, same), kv(vcol, nxt),
            tab(same), tab(same), tab(prev), tab(prev), tab(nxt), tab(nxt),
            pl.BlockSpec(memory_space=pltpu.SMEM),
        ],
        out_specs=pl.BlockSpec((1, blk, qw), lambda bi, j: (bi, j, 0)),
        out_shape=jax.ShapeDtypeStruct((b, s, qw), F32),
        compiler_params=_cparams("parallel", "parallel"),
    )(z, z, z, z, z, z, z, z, z, cos, sin, cos, sin, cos, sin, sink)


def _ab_out_kernel(of_ref, ob_ref, g_ref, sw_ref, h_ref, m_ref, ng_ref, w_ref, o_ref):
    o = of_ref[...] + ob_ref[...]
    ng = ng_ref[...]
    parts = []
    for hh in range(HG_HEADS):
        oh = o[:, hh * HG_HEAD_DIM:(hh + 1) * HG_HEAD_DIM]
        ms = jnp.mean(oh * oh, axis=-1, keepdims=True)
        parts.append(oh * lax.rsqrt(ms + EPS) * ng)
    hg = jnp.concatenate(parts, axis=1) * _silu(g_ref[0])
    mix = jnp.concatenate([hg, sw_ref[0]], axis=1).astype(BF16)
    y = jnp.dot(mix, w_ref[...], preferred_element_type=F32)
    o_ref[0] = h_ref[0] + m_ref[0, 0][2:3] * y


def _ab_out(o, z, sw, h, mod, ng, w, ctx_blocks):
    b, s, d = h.shape
    tm = ROW_BLOCK
    sel = lambda i: jnp.minimum(i // ctx_blocks, 1)
    return pl.pallas_call(
        _ab_out_kernel,
        grid=(b, s // tm),
        in_specs=[
            pl.BlockSpec((None, None, tm, HG_WIDTH), lambda bi, i: (0, bi, i, 0)),
            pl.BlockSpec((None, None, tm, HG_WIDTH), lambda bi, i: (1, bi, i, 0)),
            pl.BlockSpec((1, tm, HG_WIDTH), lambda bi, i: (bi, i, 4)),
            pl.BlockSpec((1, tm, sw.shape[-1]), lambda bi, i: (bi, i, 0)),
            pl.BlockSpec((1, tm, d), lambda bi, i: (bi, i, 0)),
            pl.BlockSpec((1, 1, 8, d), lambda bi, i: (bi, sel(i), 0, 0)),
            pl.BlockSpec((1, HG_HEAD_DIM), lambda bi, i: (0, 0)),
            pl.BlockSpec(w.shape, lambda bi, i: (0, 0)),
        ],
        out_specs=pl.BlockSpec((1, tm, d), lambda bi, i: (bi, i, 0)),
        out_shape=jax.ShapeDtypeStruct((b, s, d), F32),
        compiler_params=_cparams("parallel", "parallel"),
    )(o, o, z, sw, h, mod, ng.reshape(1, HG_HEAD_DIM), w)


def _ga_prep_kernel(z_ref, cos_ref, sin_ref, qg_ref, kg_ref, q_ref, k_ref, v_ref):
    z = z_ref[0]
    cos, sin = cos_ref[...], sin_ref[...]
    hd = GA_HEAD_DIM

    def norm_rope(x, g):
        ms = jnp.mean(x * x, axis=-1, keepdims=True)
        return _rope_half(x * lax.rsqrt(ms + EPS) * g, cos, sin, hd // 2)

    qg, kg = qg_ref[...], kg_ref[...]
    qs = [norm_rope(z[:, i * hd:(i + 1) * hd], qg) * (hd ** -0.5) for i in range(GA_Q_HEADS)]
    q_ref[0] = jnp.concatenate(qs, axis=1).astype(BF16)
    k0 = GA_Q_HEADS * hd
    ks = [norm_rope(z[:, k0 + i * hd:k0 + (i + 1) * hd], kg) for i in range(GA_KV_HEADS)]
    k_ref[0] = jnp.concatenate(ks, axis=1).astype(BF16)
    v_ref[0] = z[:, k0 + GA_KV_HEADS * hd:].astype(BF16)


def _ga_prep(z, cos, sin, qg, kg):
    b, s, n = z.shape
    tm = ROW_BLOCK
    qw, kw = GA_Q_HEADS * GA_HEAD_DIM, GA_KV_HEADS * GA_HEAD_DIM
    row = lambda width: pl.BlockSpec((1, tm, width), lambda bi, i: (bi, i, 0))
    return pl.pallas_call(
        _ga_prep_kernel,
        grid=(b, s // tm),
        in_specs=[
            row(n),
            pl.BlockSpec((tm, GA_HEAD_DIM), lambda bi, i: (i, 0)),
            pl.BlockSpec((tm, GA_HEAD_DIM), lambda bi, i: (i, 0)),
            pl.BlockSpec((1, GA_HEAD_DIM), lambda bi, i: (0, 0)),
            pl.BlockSpec((1, GA_HEAD_DIM), lambda bi, i: (0, 0)),
        ],
        out_specs=[row(qw), row(kw), row(kw)],
        out_shape=[jax.ShapeDtypeStruct((b, s, qw), BF16),
                   jax.ShapeDtypeStruct((b, s, kw), BF16),
                   jax.ShapeDtypeStruct((b, s, kw), BF16)],
        compiler_params=_cparams("parallel", "parallel"),
    )(z, cos, sin, qg.reshape(1, -1), kg.reshape(1, -1))


_GA_TQ = 128
_GA_CHUNKS = 2


def _flash_kernel(q_ref, k_ref, v_ref, o_ref, s_ref):
    hd = GA_HEAD_DIM
    group = GA_Q_HEADS // GA_KV_HEADS
    q = q_ref[0]
    tq = q.shape[0]
    qs = jnp.concatenate([q[:, i * hd:(i + 1) * hd] for i in range(group)], axis=0)
    nk = k_ref.shape[1]
    kc = nk // _GA_CHUNKS
    for c in range(_GA_CHUNKS):
        s_ref[:, c * kc:(c + 1) * kc] = lax.dot_general(qs, k_ref[0, c * kc:(c + 1) * kc, :], _NT,
                                                        preferred_element_type=F32)
    tile = lambda rows, j: s_ref[rows, j * LANES:(j + 1) * LANES]
    outs = []
    for g in range(group):
        rows = slice(g * tq, (g + 1) * tq)
        m = tile(rows, 0)
        for j in range(1, nk // LANES):
            m = jnp.maximum(m, tile(rows, j))
        m = jnp.broadcast_to(jnp.max(m, axis=-1, keepdims=True), m.shape)
        lsum = jnp.zeros_like(m)
        ps = []
        for j in range(nk // LANES):
            pj = jnp.exp(tile(rows, j) - m)
            lsum = lsum + pj
            ps.append(pj.astype(BF16))
        o = jnp.dot(jnp.concatenate(ps, axis=1), v_ref[0], preferred_element_type=F32)
        outs.append(o / jnp.sum(lsum, axis=-1, keepdims=True))
    o_ref[0] = jnp.concatenate(outs, axis=1).astype(o_ref.dtype)


def _ga_attention(q, k, v, ctx_len):
    b, s, qw = q.shape
    l = s - ctx_len
    tq = _GA_TQ
    group = GA_Q_HEADS // GA_KV_HEADS
    gw = group * GA_HEAD_DIM
    off = ctx_len // tq
    return pl.pallas_call(
        _flash_kernel,
        grid=(b, GA_KV_HEADS, l // tq),
        in_specs=[
            pl.BlockSpec((1, tq, gw), lambda bi, g, i: (bi, off + i, g)),
            pl.BlockSpec((1, s, GA_HEAD_DIM), lambda bi, g, i: (bi, 0, g)),
            pl.BlockSpec((1, s, GA_HEAD_DIM), lambda bi, g, i: (bi, 0, g)),
        ],
        out_specs=pl.BlockSpec((1, tq, gw), lambda bi, g, i: (bi, i, g)),
        out_shape=jax.ShapeDtypeStruct((b, l, qw), BF16),
        scratch_shapes=[pltpu.VMEM((group * tq, s), F32)],
        compiler_params=_cparams("parallel", "parallel", "parallel"),
    )(q, k, v)


def _out_res_kernel(a_ref, h_ref, m_ref, w_ref, o_ref):
    y = jnp.dot(a_ref[0], w_ref[...], preferred_element_type=F32)
    o_ref[0] = h_ref[0] + m_ref[0, 0][2:3] * y


def _out_res(a, h, mod, w, ctx_len):
    b, l, n = a.shape
    d = h.shape[-1]
    tm = ROW_BLOCK
    off = ctx_len // tm
    return pl.pallas_call(
        _out_res_kernel,
        grid=(b, l // tm),
        in_specs=[
            pl.BlockSpec((1, tm, n), lambda bi, i: (bi, i, 0)),
            pl.BlockSpec((1, tm, d), lambda bi, i: (bi, off + i, 0)),
            pl.BlockSpec((1, 1, 8, d), lambda bi, i: (bi, 1, 0, 0)),
            pl.BlockSpec(w.shape, lambda bi, i: (0, 0)),
        ],
        out_specs=pl.BlockSpec((1, tm, d), lambda bi, i: (bi, i, 0)),
        out_shape=jax.ShapeDtypeStruct((b, l, d), F32),
        compiler_params=_cparams("parallel", "parallel"),
    )(a, h, mod, w)


_NO_RANK = 1 << 20


def _topk_rows(s, rank=None, payload=None):
    if rank is None:
        rank = lax.broadcasted_iota(jnp.int32, s.shape, 0)
    vals, sel = [], []
    for _ in range(PEER_TOPK):
        m = jnp.max(s, axis=0, keepdims=True)
        i = jnp.min(jnp.where(s == m, rank, _NO_RANK), axis=0, keepdims=True)
        hit = rank == i
        vals.append(m)
        sel.append(i if payload is None else jnp.max(jnp.where(hit, payload, -1), axis=0, keepdims=True))
        s = jnp.where(hit, -jnp.inf, s)
    return jnp.concatenate(vals, axis=0), jnp.concatenate(sel, axis=0)


def _cand_slabs():
    k = PEER_TOPK
    slabs = [('b', 0, 0), ('b', 0, 8)] + [('b', a, 0) for a in range(1, 8)] + [('a', 8, 0)]
    rank = []
    for kind, a, b in slabs:
        for j in range(SUBLANES):
            aa, bb = (a, b + j) if kind == 'b' else (a + j, b)
            rank.append(aa * k + bb if (aa + 1) * (bb + 1) <= k else _NO_RANK)
    return slabs, np.asarray(rank, np.int32)


def _route_kernel(q_ref, keys_ref, rank_ref, ids_ref, gate_ref):
    q = q_ref[0]
    slabs, _ = _cand_slabs()
    rank = jnp.concatenate([rank_ref[...]] * (q.shape[0] // LANES), axis=1)
    ids, gates = [], []
    for h in range(PEER_HEADS):
        tops = []
        for p in range(2):
            c0 = (h * 2 + p) * PEER_HALF
            st = lax.dot_general(keys_ref[h, p], q[:, c0:c0 + PEER_HALF], _NT,
                                 precision=HIGHEST, preferred_element_type=F32)
            tops.append(_topk_rows(st))
        (s1, i1), (s2, i2) = tops
        pick = lambda x, y, kind, a, b: ((x[a:a + 1], y[b:b + SUBLANES]) if kind == 'b'
                                         else (x[a:a + SUBLANES], y[b:b + 1]))
        cand = jnp.concatenate([sum(pick(s1, s2, *sl)) for sl in slabs], axis=0)
        cand = jnp.where(rank < _NO_RANK, cand, -jnp.inf)
        eid = jnp.concatenate([(lambda x, y: x * PEER_N_KEYS + y)(*pick(i1, i2, *sl)) for sl in slabs],
                              axis=0)
        sc, e = _topk_rows(cand, rank, eid)
        ex = jnp.exp(sc - jnp.max(sc, axis=0, keepdims=True))
        gates.append(ex / jnp.sum(ex, axis=0, keepdims=True))
        ids.append(e)
    ids_ref[0] = jnp.concatenate(ids, axis=0).astype(F32).T.astype(jnp.int32)
    gate_ref[0] = jnp.concatenate(gates, axis=0).T


def _route(q, keys):
    b, r, n = q.shape
    tm = ROW_BLOCK
    rank = np.tile(_cand_slabs()[1][:, None], (1, LANES))
    return pl.pallas_call(
        _route_kernel,
        grid=(b, r // tm),
        in_specs=[
            pl.BlockSpec((1, tm, n), lambda bi, i: (bi, i, 0)),
            pl.BlockSpec(keys.shape, lambda bi, i: (0, 0, 0, 0)),
            pl.BlockSpec(rank.shape, lambda bi, i: (0, 0)),
        ],
        out_specs=[pl.BlockSpec((1, tm, PEER_PAIRS), lambda bi, i: (bi, i, 0)),
                   pl.BlockSpec((1, tm, PEER_PAIRS), lambda bi, i: (bi, i, 0))],
        out_shape=[jax.ShapeDtypeStruct((b, r, PEER_PAIRS), jnp.int32),
                   jax.ShapeDtypeStruct((b, r, PEER_PAIRS), F32)],
        compiler_params=_cparams("parallel", "parallel"),
    )(q, keys, jnp.asarray(rank))


_EX_TOK = 8
_EX_HALVES = 4
_EX_AHEAD = 2
_EX_NP = _EX_TOK * PEER_PAIRS
_ROW_TILES = 8
_ROW_PITCH = 12
_EX_GROUPS = 2 * _ROW_TILES


def _expert_kernel(ids_ref, idsn_ref, f_ref, gate_ref, h_ref, m_ref, fg_ref, ex_ref, uv_ref, o_ref,
                   *scratch, n_steps, final):
    bufs, sem = scratch[:_EX_HALVES], scratch[_EX_HALVES]
    step = pl.program_id(0) * pl.num_programs(1) + pl.program_id(1)
    per_group = _EX_NP // _EX_GROUPS

    def start_copy(e, dst, p, sem_i, priority):
        pltpu.make_async_copy(uv_ref.at[e], dst.at[pl.ds(p * _ROW_PITCH, _ROW_TILES), :],
                              sem.at[sem_i]).start(priority=priority)

    def wait_all(dst, sem_i):
        rows = pl.ds(0, _EX_NP * _ROW_TILES)
        pltpu.make_async_copy(dst.at[rows, :], dst.at[rows, :], sem.at[sem_i]).wait()

    def half(cur, half_i, issue_ids, issue_off, other, other_sem):
        rows = slice(half_i * _EX_TOK, (half_i + 1) * _EX_TOK)
        f = f_ref[0, rows, :]

        def issue(g):
            for u in range(per_group):
                p = g * per_group + u
                start_copy(issue_ids[0, issue_off + p], other, p, other_sem, u % 2)

        words = lambda r: pltpu.bitcast(cur[pl.ds(r, _EX_NP, stride=_ROW_PITCH), :], BF16)
        a = jnp.zeros((_EX_TOK, _EX_NP), F32)
        for r in range(_ROW_TILES):
            issue(r)
            u = pltpu.bitcast(lax.shift_left(cur[pl.ds(r, _EX_NP, stride=_ROW_PITCH), :], 16), F32)
            a = a + lax.dot_general(f[:, r * LANES:(r + 1) * LANES], u.astype(BF16), _NT,
                                    preferred_element_type=F32)
        act = 0.5 * a * (1.0 + lax.erf(a * (2.0 ** -0.5)))
        gate = jnp.concatenate([gate_ref[0, rows, :]] * _EX_TOK, axis=1)
        own = (lax.broadcasted_iota(jnp.int32, a.shape, 1) // PEER_PAIRS
               == lax.broadcasted_iota(jnp.int32, a.shape, 0))
        ga = jnp.where(own, gate * act, 0.0).astype(BF16)
        coef = jnp.concatenate(
            [jnp.dot(ga[:, t * PEER_PAIRS:(t + 1) * PEER_PAIRS], ex_ref[...], preferred_element_type=F32)
             for t in range(_EX_TOK)], axis=1).astype(BF16)
        ys = []
        for r in range(_ROW_TILES):
            issue(_ROW_TILES + r)
            ys.append(jnp.dot(coef, words(r), preferred_element_type=F32))
        hn = h_ref[0, rows, :] + m_ref[0, 0][5:6] * jnp.concatenate(ys, axis=1)
        if final:
            ms = jnp.mean(hn * hn, axis=-1, keepdims=True)
            hn = hn * lax.rsqrt(ms + EPS) * fg_ref[...]
        o_ref[0, rows, :] = hn

    @pl.when(step == 0)
    def _():
        for k in range(_EX_AHEAD):
            def body(p, _, k=k):
                start_copy(ids_ref[0, k * _EX_NP + p], bufs[k], p, k, 0)
                return 0
            lax.fori_loop(0, _EX_NP, body, 0)

    for k in range(_EX_HALVES):
        wait_all(bufs[k], k)
        tgt = k + _EX_AHEAD
        if tgt < _EX_HALVES:
            half(bufs[k], k, ids_ref, tgt * _EX_NP, bufs[tgt], tgt)
        else:
            half(bufs[k], k, idsn_ref, (tgt - _EX_HALVES) * _EX_NP, bufs[tgt - _EX_HALVES],
                 tgt - _EX_HALVES)

    @pl.when(step == n_steps - 1)
    def _():
        for k in range(_EX_AHEAD):
            wait_all(bufs[k], k)


def _experts(ids, gates, f, h, mod, uv, final_g, *, ctx_blocks, final):
    b, r, d = h.shape
    tok = _EX_HALVES * _EX_TOK
    nb = r // tok
    n_steps = b * nb
    ids_flat = ids.reshape(n_steps, 1, tok * PEER_PAIRS)
    per_ctx = ctx_blocks * (ROW_BLOCK // tok)
    sel = (lambda i: jnp.minimum(i // per_ctx, 1)) if ctx_blocks else (lambda i: 1)
    blk = lambda width: pl.BlockSpec((1, tok, width), lambda bi, i: (bi, i, 0))
    lin = lambda bi, i: bi * nb + i
    expand = np.zeros((PEER_PAIRS, 2 * PEER_PAIRS), np.float32)
    expand[np.arange(PEER_PAIRS), 2 * np.arange(PEER_PAIRS) + 1] = 1.0
    buf = pltpu.VMEM((_EX_NP * _ROW_PITCH, LANES), jnp.int32)
    return pl.pallas_call(
        functools.partial(_expert_kernel, n_steps=n_steps, final=final),
        grid=(b, nb),
        in_specs=[
            pl.BlockSpec((None, 1, tok * PEER_PAIRS), lambda bi, i: (lin(bi, i), 0, 0),
                         memory_space=pltpu.SMEM),
            pl.BlockSpec((None, 1, tok * PEER_PAIRS),
                         lambda bi, i: (jnp.minimum(lin(bi, i) + 1, n_steps - 1), 0, 0),
                         memory_space=pltpu.SMEM),
            blk(d), blk(PEER_PAIRS), blk(d),
            pl.BlockSpec((1, 1, 8, d), lambda bi, i: (bi, sel(i), 0, 0)),
            pl.BlockSpec((1, d), lambda bi, i: (0, 0)),
            pl.BlockSpec(expand.shape, lambda bi, i: (0, 0)),
            pl.BlockSpec(memory_space=pl.ANY),
        ],
        out_specs=blk(d),
        out_shape=jax.ShapeDtypeStruct((b, r, d), F32),
        scratch_shapes=[buf] * _EX_HALVES + [pltpu.SemaphoreType.DMA((_EX_HALVES,))],
        compiler_params=_cparams("arbitrary", "arbitrary"),
    )(ids_flat, ids_flat, f, gates, h, mod, final_g.reshape(1, d), jnp.asarray(expand, BF16), uv)


def _pack_experts(u, v):
    ub = lax.bitcast_convert_type(u.astype(BF16), jnp.uint16).astype(jnp.uint32)
    vb = lax.bitcast_convert_type(v.astype(BF16), jnp.uint16).astype(jnp.uint32)
    words = lax.bitcast_convert_type(ub | (vb << 16), jnp.int32)
    return words.reshape(u.shape[0], _ROW_TILES, LANES)


def _rope_tables(seq, ctx_len, head_dim):
    n_rows = seq // GRID_W
    row = jnp.repeat(jnp.arange(n_rows), GRID_W).astype(F32)
    col = jnp.tile(jnp.arange(GRID_W), n_rows).astype(F32)
    n_freq = head_dim // 4
    inv = ROPE_THETA ** (-jnp.arange(n_freq, dtype=F32) / n_freq)
    ang = jnp.concatenate([row[:, None] * inv, col[:, None] * inv], axis=-1)
    cos = jnp.concatenate([jnp.cos(ang), jnp.cos(ang)], axis=-1)
    sin = jnp.concatenate([-jnp.sin(ang), jnp.sin(ang)], axis=-1)
    reps = LANES // head_dim
    cos, sin = jnp.tile(cos, (1, reps)), jnp.tile(sin, (1, reps))
    cos = jnp.concatenate([jnp.ones((ctx_len, LANES), F32), cos], axis=0)
    sin = jnp.concatenate([jnp.zeros((ctx_len, LANES), F32), sin], axis=0)
    return cos, sin


def kernel(x, c, ctx, c_ctx, ada_w, ada_b, norm_mix_g, norm_ffn_g, ab_w_in, ab_w_out, hg_lb_logits,
           hg_norm_g, sw_sink, ga_w_in, ga_w_out, ga_q_norm_g, ga_k_norm_g, peer_w_q, peer_sub_keys,
           peer_u, peer_v, final_norm_g):
    b, l, d = x.shape
    lc = ctx.shape[1]
    assert lc % ROW_BLOCK == 0 and l % ROW_BLOCK == 0 and ada_w.shape[0] == 2
    ctx_blocks = lc // ROW_BLOCK
    h = jnp.concatenate([ctx, x], axis=1)

    n_mod = -(-(b + 1) // SUBLANES) * SUBLANES
    cc = jnp.zeros((n_mod, d), F32).at[:b].set(c).at[b].set(c_ctx)
    mod_all = _ada(cc, ada_w, ada_b)

    def mod_for(layer):
        lat = mod_all[layer, :b].reshape(b, 1, 6, d)
        cx = jnp.broadcast_to(mod_all[layer, b].reshape(1, 1, 6, d), (b, 1, 6, d))
        m = jnp.concatenate([cx, lat], axis=1)
        return jnp.pad(m, ((0, 0), (0, 0), (0, 2), (0, 0)))

    lbs = jnp.cumsum(jax.nn.softmax(hg_lb_logits.astype(F32), axis=0), axis=0)

    m0 = mod_for(0)
    z = _nmm(h, m0, norm_mix_g[0], ab_w_in[0].astype(BF16), row0=0, ctx_blocks=ctx_blocks)
    o = _hgrn(z, lbs[0], lc)
    cos_sw, sin_sw = _rope_tables(l, lc, SW_HEAD_DIM)
    sw = _sw_attention(z, cos_sw, sin_sw, sw_sink[0], lc)
    h = _ab_out(o, z, sw, h, m0, hg_norm_g[0], ab_w_out[0].astype(BF16), ctx_blocks)
    q, f = _nmm(h, m0, norm_ffn_g[0], peer_w_q[0].astype(BF16), row0=3, ctx_blocks=ctx_blocks,
                want_f=True)
    ids, gates = _route(q, peer_sub_keys[0])
    h = _experts(ids, gates, f, h, m0, _pack_experts(peer_u[0], peer_v[0]), final_norm_g,
                 ctx_blocks=ctx_blocks, final=False)

    m1 = mod_for(1)
    z = _nmm(h, m1, norm_mix_g[1], ga_w_in[0].astype(BF16), row0=0, ctx_blocks=ctx_blocks)
    cos_ga, sin_ga = _rope_tables(l, lc, GA_HEAD_DIM)
    qg, kg, vg = _ga_prep(z, cos_ga, sin_ga, ga_q_norm_g[0], ga_k_norm_g[0])
    att = _ga_attention(qg, kg, vg, lc)
    hl = _out_res(att, h, m1, ga_w_out[0].astype(BF16), lc)
    q, f = _nmm(hl, m1, norm_ffn_g[1], peer_w_q[1].astype(BF16), row0=3, ctx_blocks=0, want_f=True)
    ids, gates = _route(q, peer_sub_keys[1])
    return _experts(ids, gates, f, hl, m1, _pack_experts(peer_u[1], peer_v[1]), final_norm_g,
                    ctx_blocks=0, final=True)
```

```python
import functools

import numpy as np
import jax
import jax.numpy as jnp
from jax import lax
from jax.experimental import pallas as pl
from jax.experimental.pallas import tpu as pltpu

F32 = jnp.float32
BF16 = jnp.bfloat16
HIGHEST = lax.Precision.HIGHEST

EPS = 1e-6
NEG_INF = -1e30
ROPE_THETA = 10000.0
GRID_W = 64

HG_HEAD_DIM = 128
HG_HEADS = 4
HG_WIDTH = HG_HEADS * HG_HEAD_DIM
HG_CHUNK = 64
SW_HEAD_DIM = 64
SW_Q_HEADS = 8
SW_KV_HEADS = 2
WINDOW = 128
ATT_BLOCK = 128
GA_HEAD_DIM = 128
GA_Q_HEADS = 8
GA_KV_HEADS = 2
PEER_HEADS = 8
PEER_N_KEYS = 128
PEER_HALF = 128
PEER_TOPK = 16
PEER_PAIRS = PEER_HEADS * PEER_TOPK

LANES = 128
SUBLANES = 8
ROW_BLOCK = 256
VMEM_LIMIT = 56 * 1024 * 1024

_NT = (((1,), (1,)), ((), ()))
_TN = (((0,), (0,)), ((), ()))


def _cparams(*sem):
    return pltpu.CompilerParams(dimension_semantics=sem, vmem_limit_bytes=VMEM_LIMIT)


def _sigmoid(x):
    return 1.0 / (1.0 + jnp.exp(-x))


def _silu(x):
    return x * _sigmoid(x)


def _ada_kernel(c_ref, w_ref, b_ref, o_ref):
    a = _silu(c_ref[...])
    o_ref[0] = jnp.dot(a, w_ref[0], precision=HIGHEST, preferred_element_type=F32) + b_ref[0]


def _ada(cc, ada_w, ada_b):
    depth, d, n = ada_w.shape
    rows = cc.shape[0]
    tn = 1536
    return pl.pallas_call(
        _ada_kernel,
        grid=(depth, n // tn),
        in_specs=[
            pl.BlockSpec((rows, d), lambda l, j: (0, 0)),
            pl.BlockSpec((1, d, tn), lambda l, j: (l, 0, j)),
            pl.BlockSpec((1, 1, tn), lambda l, j: (l, 0, j)),
        ],
        out_specs=pl.BlockSpec((1, rows, tn), lambda l, j: (l, 0, j)),
        out_shape=jax.ShapeDtypeStruct((depth, rows, n), F32),
        compiler_params=_cparams("parallel", "parallel"),
    )(cc, ada_w, ada_b.reshape(depth, 1, n))


def _nmm_kernel(h_ref, m_ref, g_ref, w_ref, o_ref, *f_ref, row0):
    x = h_ref[0]
    ms = jnp.mean(x * x, axis=-1, keepdims=True)
    y = x * lax.rsqrt(ms + EPS) * g_ref[...]
    m = m_ref[0, 0]
    a = (y * (1.0 + m[row0 + 1:row0 + 2]) + m[row0:row0 + 1]).astype(BF16)
    if f_ref:
        f_ref[0][0] = a
    o_ref[0] = jnp.dot(a, w_ref[...], preferred_element_type=F32)


def _nmm(h, mod, g, w, *, row0, ctx_blocks, want_f=False):
    b, r, d = h.shape
    n = w.shape[1]
    tm = ROW_BLOCK
    sel = (lambda i: jnp.minimum(i // ctx_blocks, 1)) if ctx_blocks else (lambda i: 1)
    out_shape = [jax.ShapeDtypeStruct((b, r, n), F32)]
    out_specs = [pl.BlockSpec((1, tm, n), lambda bi, i: (bi, i, 0))]
    if want_f:
        out_shape.append(jax.ShapeDtypeStruct((b, r, d), BF16))
        out_specs.append(pl.BlockSpec((1, tm, d), lambda bi, i: (bi, i, 0)))
    res = pl.pallas_call(
        functools.partial(_nmm_kernel, row0=row0),
        grid=(b, r // tm),
        in_specs=[
            pl.BlockSpec((1, tm, d), lambda bi, i: (bi, i, 0)),
            pl.BlockSpec((1, 1, 8, d), lambda bi, i: (bi, sel(i), 0, 0)),
            pl.BlockSpec((1, d), lambda bi, i: (0, 0)),
            pl.BlockSpec((d, n), lambda bi, i: (0, 0)),
        ],
        out_specs=out_specs,
        out_shape=out_shape,
        compiler_params=_cparams("parallel", "parallel"),
    )(h, mod, g.reshape(1, d), w)
    return res if want_f else res[0]


_HG_LEVELS = (32, 16, 8, 4, 2, 1)
_HG_STACK_ROWS = HG_CHUNK * (2 + 2 * len(_HG_LEVELS)) + SUBLANES


def _hg_tables():
    c = HG_CHUNK
    stacks, masks = [], []
    for rev in (0, 1):
        pos = np.arange(c) if rev == 0 else c - 1 - np.arange(c)
        tt, ss = pos[:, None], pos[None, :]
        rows = [(ss <= tt)]
        qrows, krows, lmask = [], [], []
        for m in _HG_LEVELS:
            same = (tt // (2 * m)) == (ss // (2 * m))
            pt, ps = tt % (2 * m), ss % (2 * m)
            qrows.append(same & (pt >= m) & (ps >= m) & (ps <= pt))
            krows.append(same & (pt < m) & (ps > pt) & (ps <= m - 1))
            lmask.append(same & (pt >= m) & (ps < m))
        lmask.append(tt == ss)
        rows += qrows + krows + [(ss > tt), np.ones((SUBLANES, c), bool)]
        stacks.append(np.tile(np.concatenate(rows, axis=0).astype(np.float32), (1, 3)))
        masks.append(np.stack(lmask).astype(np.float32))
    return np.stack(stacks), np.stack(masks)


def _hg_kernel(q_ref, f_ref, i_ref, lb_ref, w_ref, ml_ref, o_ref, st_ref, *, n_chunks):
    d = pl.program_id(0)
    t = pl.program_id(2)
    c = HG_CHUNK
    nl = len(_HG_LEVELS)

    @pl.when(t == 0)
    def _():
        st_ref[...] = jnp.zeros_like(st_ref)

    lb = lb_ref[...]
    w = w_ref[...]

    def chunk(ci, _):
        cc = jnp.where(d == 0, ci, n_chunks - 1 - ci)
        rows = pl.ds(pl.multiple_of(cc * c, c), c)
        x = f_ref[rows, :]
        k = (1.0 - lb) * _sigmoid(-x)
        lf = jnp.log(1.0 - k)
        q = _silu(q_ref[rows, :])
        v = i_ref[rows, :]
        hi = lf.astype(BF16)
        r1 = lf - hi.astype(F32)
        mid = r1.astype(BF16)
        lo = (r1 - mid.astype(F32)).astype(BF16)
        e = jnp.exp(jnp.dot(w, jnp.concatenate([hi, mid, lo], axis=0), preferred_element_type=F32))
        qe = q * e[0:c]
        kend = k * e[c * (1 + 2 * nl):c * (2 + 2 * nl)]
        tot = e[c * (2 + 2 * nl):c * (2 + 2 * nl) + 1]
        scaled = [(q * e[c * (1 + l):c * (2 + l)], k * e[c * (1 + nl + l):c * (2 + nl + l)])
                  for l in range(nl)]
        outs = []
        for hh in range(HG_HEADS):
            hs = slice(hh * HG_HEAD_DIM, (hh + 1) * HG_HEAD_DIM)
            st = st_ref[hh]
            o = lax.dot_general(qe[:, hs], st, _NT, preferred_element_type=F32)
            a = ml_ref[nl] * lax.dot_general(q[:, hs], k[:, hs], _NT, preferred_element_type=F32)
            for l in range(nl):
                a = a + ml_ref[l] * lax.dot_general(scaled[l][0][:, hs], scaled[l][1][:, hs], _NT,
                                                    preferred_element_type=F32)
            outs.append(o + jnp.dot(a, v[:, hs], preferred_element_type=F32))
            st_ref[hh] = st * tot[:, hs] + lax.dot_general(v[:, hs], kend[:, hs], _TN,
                                                           preferred_element_type=F32)
        o_ref[rows, :] = jnp.concatenate(outs, axis=1)
        return 0

    lax.fori_loop(0, n_chunks, chunk, 0)


def _hgrn(z, lb, ctx_len):
    b, s, _ = z.shape
    tb = ROW_BLOCK
    nt, nctx = s // tb, ctx_len // tb
    stack, masks = _hg_tables()

    def tblk(d, t):
        bwd = jnp.where(t < nctx, nctx - 1 - t, nt - 1 - (t - nctx))
        return jnp.where(d == 0, t, bwd)

    return pl.pallas_call(
        functools.partial(_hg_kernel, n_chunks=tb // HG_CHUNK),
        grid=(2, b, nt),
        in_specs=[
            pl.BlockSpec((None, tb, HG_WIDTH), lambda d, bi, t: (bi, tblk(d, t), 0)),
            pl.BlockSpec((None, tb, HG_WIDTH), lambda d, bi, t: (bi, tblk(d, t), 1 + d)),
            pl.BlockSpec((None, tb, HG_WIDTH), lambda d, bi, t: (bi, tblk(d, t), 3)),
            pl.BlockSpec((1, HG_WIDTH), lambda d, bi, t: (0, 0)),
            pl.BlockSpec((None, _HG_STACK_ROWS, 3 * HG_CHUNK), lambda d, bi, t: (d, 0, 0)),
            pl.BlockSpec((None, len(_HG_LEVELS) + 1, HG_CHUNK, HG_CHUNK),
                         lambda d, bi, t: (d, 0, 0, 0)),
        ],
        out_specs=pl.BlockSpec((None, None, tb, HG_WIDTH), lambda d, bi, t: (d, bi, tblk(d, t), 0)),
        out_shape=jax.ShapeDtypeStruct((2, b, s, HG_WIDTH), F32),
        scratch_shapes=[pltpu.VMEM((HG_HEADS, HG_HEAD_DIM, HG_HEAD_DIM), F32)],
        compiler_params=_cparams("parallel", "parallel", "arbitrary"),
    )(z, z, z, lb.reshape(1, HG_WIDTH), jnp.asarray(stack, BF16), jnp.asarray(masks))


def _rope_half(x, cos, sin_signed, half):
    width = x.shape[-1]
    if 2 * half == width:
        sw = pltpu.roll(x, half, 1)
    else:
        lane = lax.broadcasted_iota(jnp.int32, x.shape, 1)
        sw = jnp.where(lane % (2 * half) < half, pltpu.roll(x, width - half, 1), pltpu.roll(x, half, 1))
    return x * cos + sw * sin_signed


def _sw_kernel(q_ref, kc_ref, vc_ref, kp_ref, k0_ref, kn_ref, vp_ref, v0_ref, vn_ref,
               cq_ref, sq_ref, cp_ref, sp_ref, cn_ref, sn_ref, sink_ref, o_ref, *, ctx_len, seq):
    jb = pl.program_id(1)
    blk = ATT_BLOCK
    half = SW_HEAD_DIM // 2
    cq, sq = cq_ref[...], sq_ref[...]
    reps = q_ref.shape[-1] // LANES
    q = (_rope_half(q_ref[0], jnp.concatenate([cq] * reps, axis=1),
                    jnp.concatenate([sq] * reps, axis=1), half) * (SW_HEAD_DIM ** -0.5)).astype(BF16)
    kw = jnp.concatenate([kp_ref[0], k0_ref[0], kn_ref[0]], axis=0)
    cw = jnp.concatenate([cp_ref[...], cq, cn_ref[...]], axis=0)
    sw = jnp.concatenate([sp_ref[...], sq, sn_ref[...]], axis=0)
    kw = _rope_half(kw, cw, sw, half)
    k_all = jnp.concatenate([kc_ref[0], kw], axis=0).astype(BF16)
    v_all = jnp.concatenate([vc_ref[0], vp_ref[0], v0_ref[0], vn_ref[0]], axis=0).astype(BF16)
    nk = ctx_len + 3 * blk
    col = lax.broadcasted_iota(jnp.int32, (blk, nk), 1)
    trow = jb * blk + lax.broadcasted_iota(jnp.int32, (blk, nk), 0)
    srow = (jb - 1) * blk + (col - ctx_len)
    band = ((trow >= ctx_len) & (srow >= ctx_len) & (srow < seq) & (jnp.abs(trow - srow) <= WINDOW))
    mask = (col < ctx_len) | band
    group = SW_Q_HEADS // SW_KV_HEADS
    outs = []
    for hq in range(SW_Q_HEADS):
        g = hq // group
        qh = q[:, hq * SW_HEAD_DIM:(hq + 1) * SW_HEAD_DIM]
        kh = k_all[:, g * SW_HEAD_DIM:(g + 1) * SW_HEAD_DIM]
        vh = v_all[:, g * SW_HEAD_DIM:(g + 1) * SW_HEAD_DIM]
        s = lax.dot_general(qh, kh, _NT, preferred_element_type=F32)
        s = jnp.where(mask, s, NEG_INF)
        sk = sink_ref[hq]
        m = jnp.maximum(jnp.max(s, axis=-1, keepdims=True), sk)
        e = jnp.exp(s - m)
        den = jnp.sum(e, axis=-1, keepdims=True) + jnp.exp(sk - m)
        outs.append(jnp.dot(e.astype(BF16), vh, preferred_element_type=F32) / den)
    o_ref[0] = jnp.concatenate(outs, axis=1)


def _sw_attention(z, cos, sin, sink, ctx_len):
    b, s, _ = z.shape
    blk = ATT_BLOCK
    nb = s // blk
    qw = SW_Q_HEADS * SW_HEAD_DIM
    qcol, kcol, vcol = 5 * HG_WIDTH // qw, 5 * HG_WIDTH // LANES + 4, 5 * HG_WIDTH // LANES + 5
    prev = lambda j: jnp.maximum(j - 1, 0)
    nxt = lambda j: jnp.minimum(j + 1, nb - 1)
    kv = lambda col, f: pl.BlockSpec((1, blk, LANES), lambda bi, j: (bi, f(j), col))
    tab = lambda f: pl.BlockSpec((blk, LANES), lambda bi, j: (f(j), 0))
    same = lambda j: j
    return pl.pallas_call(
        functools.partial(_sw_kernel, ctx_len=ctx_len, seq=s),
        grid=(b, nb),
        in_specs=[
            pl.BlockSpec((1, blk, qw), lambda bi, j: (bi, j, qcol)),
            pl.BlockSpec((1, ctx_len, LANES), lambda bi, j: (bi, 0, kcol)),
            pl.BlockSpec((1, ctx_len, LANES), lambda bi, j: (bi, 0, vcol)),
            kv(kcol, prev), kv(kcol, same), kv(kcol, nxt),
            kv(vcol, prev), kv(vcol, same), kv(vcol, nxt),
            tab(same), tab(same), tab(prev), tab(prev), tab(nxt), tab(nxt),
            pl.BlockSpec(memory_space=pltpu.SMEM),
        ],
        out_specs=pl.BlockSpec((1, blk, qw), lambda bi, j: (bi, j, 0)),
        out_shape=jax.ShapeDtypeStruct((b, s, qw), F32),
        compiler_params=_cparams("parallel", "parallel"),
    )(z, z, z, z, z, z, z, z, z, cos, sin, cos, sin, cos, sin, sink)


def _ab_out_kernel(of_ref, ob_ref, g_ref, sw_ref, h_ref, m_ref, ng_ref, w_ref, o_ref):
    o = of_ref[...] + ob_ref[...]
    ng = ng_ref[...]
    parts = []
    for hh in range(HG_HEADS):
        oh = o[:, hh * HG_HEAD_DIM:(hh + 1) * HG_HEAD_DIM]
        ms = jnp.mean(oh * oh, axis=-1, keepdims=True)
        parts.append(oh * lax.rsqrt(ms + EPS) * ng)
    hg = jnp.concatenate(parts, axis=1) * _silu(g_ref[0])
    mix = jnp.concatenate([hg, sw_ref[0]], axis=1).astype(BF16)
    y = jnp.dot(mix, w_ref[...], preferred_element_type=F32)
    o_ref[0] = h_ref[0] + m_ref[0, 0][2:3] * y


def _ab_out(o, z, sw, h, mod, ng, w, ctx_blocks):
    b, s, d = h.shape
    tm = ROW_BLOCK
    sel = lambda i: jnp.minimum(i // ctx_blocks, 1)
    return pl.pallas_call(
        _ab_out_kernel,
        grid=(b, s // tm),
        in_specs=[
            pl.BlockSpec((None, None, tm, HG_WIDTH), lambda bi, i: (0, bi, i, 0)),
            pl.BlockSpec((None, None, tm, HG_WIDTH), lambda bi, i: (1, bi, i, 0)),
            pl.BlockSpec((1, tm, HG_WIDTH), lambda bi, i: (bi, i, 4)),
            pl.BlockSpec((1, tm, sw.shape[-1]), lambda bi, i: (bi, i, 0)),
            pl.BlockSpec((1, tm, d), lambda bi, i: (bi, i, 0)),
            pl.BlockSpec((1, 1, 8, d), lambda bi, i: (bi, sel(i), 0, 0)),
            pl.BlockSpec((1, HG_HEAD_DIM), lambda bi, i: (0, 0)),
            pl.BlockSpec(w.shape, lambda bi, i: (0, 0)),
        ],
        out_specs=pl.BlockSpec((1, tm, d), lambda bi, i: (bi, i, 0)),
        out_shape=jax.ShapeDtypeStruct((b, s, d), F32),
        compiler_params=_cparams("parallel", "parallel"),
    )(o, o, z, sw, h, mod, ng.reshape(1, HG_HEAD_DIM), w)


def _ga_prep_kernel(z_ref, cos_ref, sin_ref, qg_ref, kg_ref, q_ref, k_ref, v_ref):
    z = z_ref[0]
    cos, sin = cos_ref[...], sin_ref[...]
    hd = GA_HEAD_DIM

    def norm_rope(x, g):
        ms = jnp.mean(x * x, axis=-1, keepdims=True)
        return _rope_half(x * lax.rsqrt(ms + EPS) * g, cos, sin, hd // 2)

    qg, kg = qg_ref[...], kg_ref[...]
    qs = [norm_rope(z[:, i * hd:(i + 1) * hd], qg) * (hd ** -0.5) for i in range(GA_Q_HEADS)]
    q_ref[0] = jnp.concatenate(qs, axis=1).astype(BF16)
    k0 = GA_Q_HEADS * hd
    ks = [norm_rope(z[:, k0 + i * hd:k0 + (i + 1) * hd], kg) for i in range(GA_KV_HEADS)]
    k_ref[0] = jnp.concatenate(ks, axis=1).astype(BF16)
    v_ref[0] = z[:, k0 + GA_KV_HEADS * hd:].astype(BF16)


def _ga_prep(z, cos, sin, qg, kg):
    b, s, n = z.shape
    tm = ROW_BLOCK
    qw, kw = GA_Q_HEADS * GA_HEAD_DIM, GA_KV_HEADS * GA_HEAD_DIM
    row = lambda width: pl.BlockSpec((1, tm, width), lambda bi, i: (bi, i, 0))
    return pl.pallas_call(
        _ga_prep_kernel,
        grid=(b, s // tm),
        in_specs=[
            row(n),
            pl.BlockSpec((tm, GA_HEAD_DIM), lambda bi, i: (i, 0)),
            pl.BlockSpec((tm, GA_HEAD_DIM), lambda bi, i: (i, 0)),
            pl.BlockSpec((1, GA_HEAD_DIM), lambda bi, i: (0, 0)),
            pl.BlockSpec((1, GA_HEAD_DIM), lambda bi, i: (0, 0)),
        ],
        out_specs=[row(qw), row(kw), row(kw)],
        out_shape=[jax.ShapeDtypeStruct((b, s, qw), BF16),
                   jax.ShapeDtypeStruct((b, s, kw), BF16),
                   jax.ShapeDtypeStruct((b, s, kw), BF16)],
        compiler_params=_cparams("parallel", "parallel"),
    )(z, cos, sin, qg.reshape(1, -1), kg.reshape(1, -1))


_GA_TQ = 128
_GA_CHUNKS = 2


def _flash_kernel(q_ref, k_ref, v_ref, o_ref, s_ref):
    hd = GA_HEAD_DIM
    group = GA_Q_HEADS // GA_KV_HEADS
    q = q_ref[0]
    tq = q.shape[0]
    qs = jnp.concatenate([q[:, i * hd:(i + 1) * hd] for i in range(group)], axis=0)
    nk = k_ref.shape[1]
    kc = nk // _GA_CHUNKS
    for c in range(_GA_CHUNKS):
        s_ref[:, c * kc:(c + 1) * kc] = lax.dot_general(qs, k_ref[0, c * kc:(c + 1) * kc, :], _NT,
                                                        preferred_element_type=F32)
    tile = lambda rows, j: s_ref[rows, j * LANES:(j + 1) * LANES]
    outs = []
    for g in range(group):
        rows = slice(g * tq, (g + 1) * tq)
        m = tile(rows, 0)
        for j in range(1, nk // LANES):
            m = jnp.maximum(m, tile(rows, j))
        m = jnp.broadcast_to(jnp.max(m, axis=-1, keepdims=True), m.shape)
        lsum = jnp.zeros_like(m)
        ps = []
        for j in range(nk // LANES):
            pj = jnp.exp(tile(rows, j) - m)
            lsum = lsum + pj
            ps.append(pj.astype(BF16))
        o = jnp.dot(jnp.concatenate(ps, axis=1), v_ref[0], preferred_element_type=F32)
        outs.append(o / jnp.sum(lsum, axis=-1, keepdims=True))
    o_ref[0] = jnp.concatenate(outs, axis=1).astype(o_ref.dtype)


def _ga_attention(q, k, v, ctx_len):
    b, s, qw = q.shape
    l = s - ctx_len
    tq = _GA_TQ
    group = GA_Q_HEADS // GA_KV_HEADS
    gw = group * GA_HEAD_DIM
    off = ctx_len // tq
    return pl.pallas_call(
        _flash_kernel,
        grid=(b, GA_KV_HEADS, l // tq),
        in_specs=[
            pl.BlockSpec((1, tq, gw), lambda bi, g, i: (bi, off + i, g)),
            pl.BlockSpec((1, s, GA_HEAD_DIM), lambda bi, g, i: (bi, 0, g)),
            pl.BlockSpec((1, s, GA_HEAD_DIM), lambda bi, g, i: (bi, 0, g)),
        ],
        out_specs=pl.BlockSpec((1, tq, gw), lambda bi, g, i: (bi, i, g)),
        out_shape=jax.ShapeDtypeStruct((b, l, qw), BF16),
        scratch_shapes=[pltpu.VMEM((group * tq, s), F32)],
        compiler_params=_cparams("parallel", "parallel", "parallel"),
    )(q, k, v)


def _out_res_kernel(a_ref, h_ref, m_ref, w_ref, o_ref):
    y = jnp.dot(a_ref[0], w_ref[...], preferred_element_type=F32)
    o_ref[0] = h_ref[0] + m_ref[0, 0][2:3] * y


def _out_res(a, h, mod, w, ctx_len):
    b, l, n = a.shape
    d = h.shape[-1]
    tm = ROW_BLOCK
    off = ctx_len // tm
    return pl.pallas_call(
        _out_res_kernel,
        grid=(b, l // tm),
        in_specs=[
            pl.BlockSpec((1, tm, n), lambda bi, i: (bi, i, 0)),
            pl.BlockSpec((1, tm, d), lambda bi, i: (bi, off + i, 0)),
            pl.BlockSpec((1, 1, 8, d), lambda bi, i: (bi, 1, 0, 0)),
            pl.BlockSpec(w.shape, lambda bi, i: (0, 0)),
        ],
        out_specs=pl.BlockSpec((1, tm, d), lambda bi, i: (bi, i, 0)),
        out_shape=jax.ShapeDtypeStruct((b, l, d), F32),
        compiler_params=_cparams("parallel", "parallel"),
    )(a, h, mod, w)


_NO_RANK = 1 << 20


def _topk_rows(s, rank=None, payload=None):
    if rank is None:
        rank = lax.broadcasted_iota(jnp.int32, s.shape, 0)
    vals, sel = [], []
    for _ in range(PEER_TOPK):
        m = jnp.max(s, axis=0, keepdims=True)
        i = jnp.min(jnp.where(s == m, rank, _NO_RANK), axis=0, keepdims=True)
        hit = rank == i
        vals.append(m)
        sel.append(i if payload is None else jnp.max(jnp.where(hit, payload, -1), axis=0, keepdims=True))
        s = jnp.where(hit, -jnp.inf, s)
    return jnp.concatenate(vals, axis=0), jnp.concatenate(sel, axis=0)


def _cand_slabs():
    k = PEER_TOPK
    slabs = [('b', 0, 0), ('b', 0, 8)] + [('b', a, 0) for a in range(1, 8)] + [('a', 8, 0)]
    rank = []
    for kind, a, b in slabs:
        for j in range(SUBLANES):
            aa, bb = (a, b + j) if kind == 'b' else (a + j, b)
            rank.append(aa * k + bb if (aa + 1) * (bb + 1) <= k else _NO_RANK)
    return slabs, np.asarray(rank, np.int32)


def _route_kernel(q_ref, keys_ref, rank_ref, ids_ref, gate_ref):
    q = q_ref[0]
    slabs, _ = _cand_slabs()
    rank = jnp.concatenate([rank_ref[...]] * (q.shape[0] // LANES), axis=1)
    ids, gates = [], []
    for h in range(PEER_HEADS):
        tops = []
        for p in range(2):
            c0 = (h * 2 + p) * PEER_HALF
            st = lax.dot_general(keys_ref[h, p], q[:, c0:c0 + PEER_HALF], _NT,
                                 precision=HIGHEST, preferred_element_type=F32)
            tops.append(_topk_rows(st))
        (s1, i1), (s2, i2) = tops
        pick = lambda x, y, kind, a, b: ((x[a:a + 1], y[b:b + SUBLANES]) if kind == 'b'
                                         else (x[a:a + SUBLANES], y[b:b + 1]))
        cand = jnp.concatenate([sum(pick(s1, s2, *sl)) for sl in slabs], axis=0)
        cand = jnp.where(rank < _NO_RANK, cand, -jnp.inf)
        eid = jnp.concatenate([(lambda x, y: x * PEER_N_KEYS + y)(*pick(i1, i2, *sl)) for sl in slabs],
                              axis=0)
        sc, e = _topk_rows(cand, rank, eid)
        ex = jnp.exp(sc - jnp.max(sc, axis=0, keepdims=True))
        gates.append(ex / jnp.sum(ex, axis=0, keepdims=True))
        ids.append(e)
    ids_ref[0] = jnp.concatenate(ids, axis=0).astype(F32).T.astype(jnp.int32)
    gate_ref[0] = jnp.concatenate(gates, axis=0).T


def _route(q, keys):
    b, r, n = q.shape
    tm = ROW_BLOCK
    rank = np.tile(_cand_slabs()[1][:, None], (1, LANES))
    return pl.pallas_call(
        _route_kernel,
        grid=(b, r // tm),
        in_specs=[
            pl.BlockSpec((1, tm, n), lambda bi, i: (bi, i, 0)),
            pl.BlockSpec(keys.shape, lambda bi, i: (0, 0, 0, 0)),
            pl.BlockSpec(rank.shape, lambda bi, i: (0, 0)),
        ],
        out_specs=[pl.BlockSpec((1, tm, PEER_PAIRS), lambda bi, i: (bi, i, 0)),
                   pl.BlockSpec((1, tm, PEER_PAIRS), lambda bi, i: (bi, i, 0))],
        out_shape=[jax.ShapeDtypeStruct((b, r, PEER_PAIRS), jnp.int32),
                   jax.ShapeDtypeStruct((b, r, PEER_PAIRS), F32)],
        compiler_params=_cparams("parallel", "parallel"),
    )(q, keys, jnp.asarray(rank))


_EX_TOK = 8
_EX_HALVES = 4
_EX_AHEAD = 2
_EX_NP = _EX_TOK * PEER_PAIRS
_ROW_TILES = 8
_ROW_PITCH = 12
_EX_GROUPS = 2 * _ROW_TILES


def _expert_kernel(ids_ref, idsn_ref, f_ref, gate_ref, h_ref, m_ref, fg_ref, ex_ref, uv_ref, o_ref,
                   *scratch, n_steps, final):
    bufs, sem = scratch[:_EX_HALVES], scratch[_EX_HALVES]
    step = pl.program_id(0) * pl.num_programs(1) + pl.program_id(1)
    per_group = _EX_NP // _EX_GROUPS

    def start_copy(e, dst, p, sem_i, priority):
        pltpu.make_async_copy(uv_ref.at[e], dst.at[pl.ds(p * _ROW_PITCH, _ROW_TILES), :],
                              sem.at[sem_i]).start(priority=priority)

    def wait_all(dst, sem_i):
        rows = pl.ds(0, _EX_NP * _ROW_TILES)
        pltpu.make_async_copy(dst.at[rows, :], dst.at[rows, :], sem.at[sem_i]).wait()

    def half(cur, half_i, issue_ids, issue_off, other, other_sem):
        rows = slice(half_i * _EX_TOK, (half_i + 1) * _EX_TOK)
        f = f_ref[0, rows, :]

        def issue(g):
            for u in range(per_group):
                p = g * per_group + u
                start_copy(issue_ids[0, issue_off + p], other, p, other_sem, u % 2)

        words = lambda r: pltpu.bitcast(cur[pl.ds(r, _EX_NP, stride=_ROW_PITCH), :], BF16)
        a = jnp.zeros((_EX_TOK, _EX_NP), F32)
        for r in range(_ROW_TILES):
            issue(r)
            u = pltpu.bitcast(lax.shift_left(cur[pl.ds(r, _EX_NP, stride=_ROW_PITCH), :], 16), F32)
            a = a + lax.dot_general(f[:, r * LANES:(r + 1) * LANES], u.astype(BF16), _NT,
                                    preferred_element_type=F32)
        act = 0.5 * a * (1.0 + lax.erf(a * (2.0 ** -0.5)))
        gate = jnp.concatenate([gate_ref[0, rows, :]] * _EX_TOK, axis=1)
        own = (lax.broadcasted_iota(jnp.int32, a.shape, 1) // PEER_PAIRS
               == lax.broadcasted_iota(jnp.int32, a.shape, 0))
        ga = jnp.where(own, gate * act, 0.0).astype(BF16)
        coef = jnp.concatenate(
            [jnp.dot(ga[:, t * PEER_PAIRS:(t + 1) * PEER_PAIRS], ex_ref[...], preferred_element_type=F32)
             for t in range(_EX_TOK)], axis=1).astype(BF16)
        ys = []
        for r in range(_ROW_TILES):
            issue(_ROW_TILES + r)
            ys.append(jnp.dot(coef, words(r), preferred_element_type=F32))
        hn = h_ref[0, rows, :] + m_ref[0, 0][5:6] * jnp.concatenate(ys, axis=1)
        if final:
            ms = jnp.mean(hn * hn, axis=-1, keepdims=True)
            hn = hn * lax.rsqrt(ms + EPS) * fg_ref[...]
        o_ref[0, rows, :] = hn

    @pl.when(step == 0)
    def _():
        for k in range(_EX_AHEAD):
            def body(p, _, k=k):
                start_copy(ids_ref[0, k * _EX_NP + p], bufs[k], p, k, 0)
                return 0
            lax.fori_loop(0, _EX_NP, body, 0)

    for k in range(_EX_HALVES):
        wait_all(bufs[k], k)
        tgt = k + _EX_AHEAD
        if tgt < _EX_HALVES:
            half(bufs[k], k, ids_ref, tgt * _EX_NP, bufs[tgt], tgt)
        else:
            half(bufs[k], k, idsn_ref, (tgt - _EX_HALVES) * _EX_NP, bufs[tgt - _EX_HALVES],
                 tgt - _EX_HALVES)

    @pl.when(step == n_steps - 1)
    def _():
        for k in range(_EX_AHEAD):
            wait_all(bufs[k], k)


def _experts(ids, gates, f, h, mod, uv, final_g, *, ctx_blocks, final):
    b, r, d = h.shape
    tok = _EX_HALVES * _EX_TOK
    nb = r // tok
    n_steps = b * nb
    ids_flat = ids.reshape(n_steps, 1, tok * PEER_PAIRS)
    per_ctx = ctx_blocks * (ROW_BLOCK // tok)
    sel = (lambda i: jnp.minimum(i // per_ctx, 1)) if ctx_blocks else (lambda i: 1)
    blk = lambda width: pl.BlockSpec((1, tok, width), lambda bi, i: (bi, i, 0))
    lin = lambda bi, i: bi * nb + i
    expand = np.zeros((PEER_PAIRS, 2 * PEER_PAIRS), np.float32)
    expand[np.arange(PEER_PAIRS), 2 * np.arange(PEER_PAIRS) + 1] = 1.0
    buf = pltpu.VMEM((_EX_NP * _ROW_PITCH, LANES), jnp.int32)
    return pl.pallas_call(
        functools.partial(_expert_kernel, n_steps=n_steps, final=final),
        grid=(b, nb),
        in_specs=[
            pl.BlockSpec((None, 1, tok * PEER_PAIRS), lambda bi, i: (lin(bi, i), 0, 0),
                         memory_space=pltpu.SMEM),
            pl.BlockSpec((None, 1, tok * PEER_PAIRS),
                         lambda bi, i: (jnp.minimum(lin(bi, i) + 1, n_steps - 1), 0, 0),
                         memory_space=pltpu.SMEM),
            blk(d), blk(PEER_PAIRS), blk(d),
            pl.BlockSpec((1, 1, 8, d), lambda bi, i: (bi, sel(i), 0, 0)),
            pl.BlockSpec((1, d), lambda bi, i: (0, 0)),
            pl.BlockSpec(expand.shape, lambda bi, i: (0, 0)),
            pl.BlockSpec(memory_space=pl.ANY),
        ],
        out_specs=blk(d),
        out_shape=jax.ShapeDtypeStruct((b, r, d), F32),
        scratch_shapes=[buf] * _EX_HALVES + [pltpu.SemaphoreType.DMA((_EX_HALVES,))],
        compiler_params=_cparams("arbitrary", "arbitrary"),
    )(ids_flat, ids_flat, f, gates, h, mod, final_g.reshape(1, d), jnp.asarray(expand, BF16), uv)


def _pack_experts(u, v):
    ub = lax.bitcast_convert_type(u.astype(BF16), jnp.uint16).astype(jnp.uint32)
    vb = lax.bitcast_convert_type(v.astype(BF16), jnp.uint16).astype(jnp.uint32)
    words = lax.bitcast_convert_type(ub | (vb << 16), jnp.int32)
    return words.reshape(u.shape[0], _ROW_TILES, LANES)


def _rope_tables(seq, ctx_len, head_dim):
    n_rows = seq // GRID_W
    row = jnp.repeat(jnp.arange(n_rows), GRID_W).astype(F32)
    col = jnp.tile(jnp.arange(GRID_W), n_rows).astype(F32)
    n_freq = head_dim // 4
    inv = ROPE_THETA ** (-jnp.arange(n_freq, dtype=F32) / n_freq)
    ang = jnp.concatenate([row[:, None] * inv, col[:, None] * inv], axis=-1)
    cos = jnp.concatenate([jnp.cos(ang), jnp.cos(ang)], axis=-1)
    sin = jnp.concatenate([-jnp.sin(ang), jnp.sin(ang)], axis=-1)
    reps = LANES // head_dim
    cos, sin = jnp.tile(cos, (1, reps)), jnp.tile(sin, (1, reps))
    cos = jnp.concatenate([jnp.ones((ctx_len, LANES), F32), cos], axis=0)
    sin = jnp.concatenate([jnp.zeros((ctx_len, LANES), F32), sin], axis=0)
    return cos, sin


def kernel(x, c, ctx, c_ctx, ada_w, ada_b, norm_mix_g, norm_ffn_g, ab_w_in, ab_w_out, hg_lb_logits,
           hg_norm_g, sw_sink, ga_w_in, ga_w_out, ga_q_norm_g, ga_k_norm_g, peer_w_q, peer_sub_keys,
           peer_u, peer_v, final_norm_g):
    b, l, d = x.shape
    lc = ctx.shape[1]
    assert lc % ROW_BLOCK == 0 and l % ROW_BLOCK == 0 and ada_w.shape[0] == 2
    ctx_blocks = lc // ROW_BLOCK
    h = jnp.concatenate([ctx, x], axis=1)

    n_mod = -(-(b + 1) // SUBLANES) * SUBLANES
    cc = jnp.zeros((n_mod, d), F32).at[:b].set(c).at[b].set(c_ctx)
    mod_all = _ada(cc, ada_w, ada_b)

    def mod_for(layer):
        lat = mod_all[layer, :b].reshape(b, 1, 6, d)
        cx = jnp.broadcast_to(mod_all[layer, b].reshape(1, 1, 6, d), (b, 1, 6, d))
        m = jnp.concatenate([cx, lat], axis=1)
        return jnp.pad(m, ((0, 0), (0, 0), (0, 2), (0, 0)))

    lbs = jnp.cumsum(jax.nn.softmax(hg_lb_logits.astype(F32), axis=0), axis=0)

    m0 = mod_for(0)
    z = _nmm(h, m0, norm_mix_g[0], ab_w_in[0].astype(BF16), row0=0, ctx_blocks=ctx_blocks)
    o = _hgrn(z, lbs[0], lc)
    cos_sw, sin_sw = _rope_tables(l, lc, SW_HEAD_DIM)
    sw = _sw_attention(z, cos_sw, sin_sw, sw_sink[0], lc)
    h = _ab_out(o, z, sw, h, m0, hg_norm_g[0], ab_w_out[0].astype(BF16), ctx_blocks)
    q, f = _nmm(h, m0, norm_ffn_g[0], peer_w_q[0].astype(BF16), row0=3, ctx_blocks=ctx_blocks,
                want_f=True)
    ids, gates = _route(q, peer_sub_keys[0])
    h = _experts(ids, gates, f, h, m0, _pack_experts(peer_u[0], peer_v[0]), final_norm_g,
                 ctx_blocks=ctx_blocks, final=False)

    m1 = mod_for(1)
    z = _nmm(h, m1, norm_mix_g[1], ga_w_in[0].astype(BF16), row0=0, ctx_blocks=ctx_blocks)
    cos_ga, sin_ga = _rope_tables(l, lc, GA_HEAD_DIM)
    qg, kg, vg = _ga_prep(z, cos_ga, sin_ga, ga_q_norm_g[0], ga_k_norm_g[0])
    att = _ga_attention(qg, kg, vg, lc)
    hl = _out_res(att, h, m1, ga_w_out[0].astype(BF16), lc)
    q, f = _nmm(hl, m1, norm_ffn_g[1], peer_w_q[1].astype(BF16), row0=3, ctx_blocks=0, want_f=True)
    ids, gates = _route(q, peer_sub_keys[1])
    return _experts(ids, gates, f, hl, m1, _pack_experts(peer_u[1], peer_v[1]), final_norm_g,
                    ctx_blocks=0, final=True)
```

```python
import functools

import numpy as np
import jax
import jax.numpy as jnp
from jax import lax
from jax.experimental import pallas as pl
from jax.experimental.pallas import tpu as pltpu

F32 = jnp.float32
BF16 = jnp.bfloat16
HIGHEST = lax.Precision.HIGHEST

EPS = 1e-6
NEG_INF = -1e30
ROPE_THETA = 10000.0
GRID_W = 64

HG_HEAD_DIM = 128
HG_HEADS = 4
HG_WIDTH = HG_HEADS * HG_HEAD_DIM
HG_CHUNK = 64
SW_HEAD_DIM = 64
SW_Q_HEADS = 8
SW_KV_HEADS = 2
WINDOW = 128
ATT_BLOCK = 128
GA_HEAD_DIM = 128
GA_Q_HEADS = 8
GA_KV_HEADS = 2
PEER_HEADS = 8
PEER_N_KEYS = 128
PEER_HALF = 128
PEER_TOPK = 16
PEER_PAIRS = PEER_HEADS * PEER_TOPK

LANES = 128
SUBLANES = 8
ROW_BLOCK = 256
VMEM_LIMIT = 56 * 1024 * 1024

_NT = (((1,), (1,)), ((), ()))
_TN = (((0,), (0,)), ((), ()))


def _cparams(*sem):
    return pltpu.CompilerParams(dimension_semantics=sem, vmem_limit_bytes=VMEM_LIMIT)


def _sigmoid(x):
    return 1.0 / (1.0 + jnp.exp(-x))


def _silu(x):
    return x * _sigmoid(x)


def _ada_kernel(c_ref, w_ref, b_ref, o_ref):
    a = _silu(c_ref[...])
    o_ref[0] = jnp.dot(a, w_ref[0], precision=HIGHEST, preferred_element_type=F32) + b_ref[0]


def _ada(cc, ada_w, ada_b):
    depth, d, n = ada_w.shape
    rows = cc.shape[0]
    tn = 1536
    return pl.pallas_call(
        _ada_kernel,
        grid=(depth, n // tn),
        in_specs=[
            pl.BlockSpec((rows, d), lambda l, j: (0, 0)),
            pl.BlockSpec((1, d, tn), lambda l, j: (l, 0, j)),
            pl.BlockSpec((1, 1, tn), lambda l, j: (l, 0, j)),
        ],
        out_specs=pl.BlockSpec((1, rows, tn), lambda l, j: (l, 0, j)),
        out_shape=jax.ShapeDtypeStruct((depth, rows, n), F32),
        compiler_params=_cparams("parallel", "parallel"),
    )(cc, ada_w, ada_b.reshape(depth, 1, n))


def _nmm_kernel(h_ref, m_ref, g_ref, w_ref, o_ref, *f_ref, row0):
    x = h_ref[0]
    ms = jnp.mean(x * x, axis=-1, keepdims=True)
    y = x * lax.rsqrt(ms + EPS) * g_ref[...]
    m = m_ref[0, 0]
    a = (y * (1.0 + m[row0 + 1:row0 + 2]) + m[row0:row0 + 1]).astype(BF16)
    if f_ref:
        f_ref[0][0] = a
    o_ref[0] = jnp.dot(a, w_ref[...], preferred_element_type=F32)


def _nmm(h, mod, g, w, *, row0, ctx_blocks, want_f=False):
    b, r, d = h.shape
    n = w.shape[1]
    tm = ROW_BLOCK
    sel = (lambda i: jnp.minimum(i // ctx_blocks, 1)) if ctx_blocks else (lambda i: 1)
    out_shape = [jax.ShapeDtypeStruct((b, r, n), F32)]
    out_specs = [pl.BlockSpec((1, tm, n), lambda bi, i: (bi, i, 0))]
    if want_f:
        out_shape.append(jax.ShapeDtypeStruct((b, r, d), BF16))
        out_specs.append(pl.BlockSpec((1, tm, d), lambda bi, i: (bi, i, 0)))
    res = pl.pallas_call(
        functools.partial(_nmm_kernel, row0=row0),
        grid=(b, r // tm),
        in_specs=[
            pl.BlockSpec((1, tm, d), lambda bi, i: (bi, i, 0)),
            pl.BlockSpec((1, 1, 8, d), lambda bi, i: (bi, sel(i), 0, 0)),
            pl.BlockSpec((1, d), lambda bi, i: (0, 0)),
            pl.BlockSpec((d, n), lambda bi, i: (0, 0)),
        ],
        out_specs=out_specs,
        out_shape=out_shape,
        compiler_params=_cparams("parallel", "parallel"),
    )(h, mod, g.reshape(1, d), w)
    return res if want_f else res[0]


_HG_LEVELS = (32, 16, 8, 4, 2, 1)
_HG_STACK_ROWS = HG_CHUNK * (2 + 2 * len(_HG_LEVELS)) + SUBLANES


def _hg_tables():
    c = HG_CHUNK
    stacks, masks = [], []
    for rev in (0, 1):
        pos = np.arange(c) if rev == 0 else c - 1 - np.arange(c)
        tt, ss = pos[:, None], pos[None, :]
        rows = [(ss <= tt)]
        qrows, krows, lmask = [], [], []
        for m in _HG_LEVELS:
            same = (tt // (2 * m)) == (ss // (2 * m))
            pt, ps = tt % (2 * m), ss % (2 * m)
            qrows.append(same & (pt >= m) & (ps >= m) & (ps <= pt))
            krows.append(same & (pt < m) & (ps > pt) & (ps <= m - 1))
            lmask.append(same & (pt >= m) & (ps < m))
        lmask.append(tt == ss)
        rows += qrows + krows + [(ss > tt), np.ones((SUBLANES, c), bool)]
        stacks.append(np.tile(np.concatenate(rows, axis=0).astype(np.float32), (1, 3)))
        masks.append(np.stack(lmask).astype(np.float32))
    return np.stack(stacks), np.stack(masks)


def _hg_kernel(qf_ref, ff_ref, if_ref, qb_ref, fb_ref, ib_ref, lb_ref, w_ref, ml_ref, of_ref, ob_ref,
               st_ref, *, n_chunks):
    t = pl.program_id(1)

    @pl.when(t == 0)
    def _():
        st_ref[...] = jnp.zeros_like(st_ref)

    def chunk(ci, _):
        _hg_chunk(0, ci, qf_ref, ff_ref, if_ref, lb_ref, w_ref, ml_ref, of_ref, st_ref)
        _hg_chunk(1, n_chunks - 1 - ci, qb_ref, fb_ref, ib_ref, lb_ref, w_ref, ml_ref, ob_ref, st_ref)
        return 0

    lax.fori_loop(0, n_chunks, chunk, 0)


def _hg_chunk(d, cc, q_ref, f_ref, i_ref, lb_ref, w_ref, ml_ref, o_ref, st_ref):
    c = HG_CHUNK
    nl = len(_HG_LEVELS)
    lb = lb_ref[...]
    w = w_ref[d]
    rows = pl.ds(pl.multiple_of(cc * c, c), c)
    x = f_ref[rows, :]
    k = (1.0 - lb) * _sigmoid(-x)
    lf = jnp.log(1.0 - k)
    q = _silu(q_ref[rows, :])
    v = i_ref[rows, :]
    hi = lf.astype(BF16)
    r1 = lf - hi.astype(F32)
    mid = r1.astype(BF16)
    lo = (r1 - mid.astype(F32)).astype(BF16)
    e = jnp.exp(jnp.dot(w, jnp.concatenate([hi, mid, lo], axis=0), preferred_element_type=F32))
    qe = q * e[0:c]
    kend = k * e[c * (1 + 2 * nl):c * (2 + 2 * nl)]
    tot = e[c * (2 + 2 * nl):c * (2 + 2 * nl) + 1]
    scaled = [(q * e[c * (1 + l):c * (2 + l)], k * e[c * (1 + nl + l):c * (2 + nl + l)])
              for l in range(nl)]
    outs = []
    for hh in range(HG_HEADS):
        hs = slice(hh * HG_HEAD_DIM, (hh + 1) * HG_HEAD_DIM)
        st = st_ref[d, hh]
        o = lax.dot_general(qe[:, hs], st, _NT, preferred_element_type=F32)
        a = ml_ref[d, nl] * lax.dot_general(q[:, hs], k[:, hs], _NT, preferred_element_type=F32)
        for l in range(nl):
            a = a + ml_ref[d, l] * lax.dot_general(scaled[l][0][:, hs], scaled[l][1][:, hs], _NT,
                                                   preferred_element_type=F32)
        outs.append(o + jnp.dot(a, v[:, hs], preferred_element_type=F32))
        st_ref[d, hh] = st * tot[:, hs] + lax.dot_general(v[:, hs], kend[:, hs], _TN,
                                                          preferred_element_type=F32)
    o_ref[rows, :] = jnp.concatenate(outs, axis=1)


def _hgrn(z, lb, ctx_len):
    b, s, _ = z.shape
    tb = ROW_BLOCK
    nt, nctx = s // tb, ctx_len // tb
    stack, masks = _hg_tables()

    bwd = lambda t: jnp.where(t < nctx, nctx - 1 - t, nt - 1 - (t - nctx))
    fcol = lambda col: pl.BlockSpec((None, tb, HG_WIDTH), lambda bi, t: (bi, t, col))
    bcol = lambda col: pl.BlockSpec((None, tb, HG_WIDTH), lambda bi, t: (bi, bwd(t), col))
    return pl.pallas_call(
        functools.partial(_hg_kernel, n_chunks=tb // HG_CHUNK),
        grid=(b, nt),
        in_specs=[
            fcol(0), fcol(1), fcol(3), bcol(0), bcol(2), bcol(3),
            pl.BlockSpec((1, HG_WIDTH), lambda bi, t: (0, 0)),
            pl.BlockSpec(stack.shape, lambda bi, t: (0, 0, 0)),
            pl.BlockSpec(masks.shape, lambda bi, t: (0, 0, 0, 0)),
        ],
        out_specs=[fcol(0), bcol(0)],
        out_shape=[jax.ShapeDtypeStruct((b, s, HG_WIDTH), F32)] * 2,
        scratch_shapes=[pltpu.VMEM((2, HG_HEADS, HG_HEAD_DIM, HG_HEAD_DIM), F32)],
        compiler_params=_cparams("parallel", "arbitrary"),
    )(z, z, z, z, z, z, lb.reshape(1, HG_WIDTH), jnp.asarray(stack, BF16), jnp.asarray(masks))


def _rope_half(x, cos, sin_signed, half):
    width = x.shape[-1]
    if 2 * half == width:
        sw = pltpu.roll(x, half, 1)
    else:
        lane = lax.broadcasted_iota(jnp.int32, x.shape, 1)
        sw = jnp.where(lane % (2 * half) < half, pltpu.roll(x, width - half, 1), pltpu.roll(x, half, 1))
    return x * cos + sw * sin_signed


def _sw_kernel(q_ref, kc_ref, vc_ref, kp_ref, k0_ref, kn_ref, vp_ref, v0_ref, vn_ref,
               cq_ref, sq_ref, cp_ref, sp_ref, cn_ref, sn_ref, sink_ref, o_ref, *, ctx_len, seq):
    jb = pl.program_id(1)
    blk = ATT_BLOCK
    half = SW_HEAD_DIM // 2
    cq, sq = cq_ref[...], sq_ref[...]
    reps = q_ref.shape[-1] // LANES
    q = (_rope_half(q_ref[0], jnp.concatenate([cq] * reps, axis=1),
                    jnp.concatenate([sq] * reps, axis=1), half) * (SW_HEAD_DIM ** -0.5)).astype(BF16)
    kw = jnp.concatenate([kp_ref[0], k0_ref[0], kn_ref[0]], axis=0)
    cw = jnp.concatenate([cp_ref[...], cq, cn_ref[...]], axis=0)
    sw = jnp.concatenate([sp_ref[...], sq, sn_ref[...]], axis=0)
    kw = _rope_half(kw, cw, sw, half)
    k_all = jnp.concatenate([kc_ref[0], kw], axis=0).astype(BF16)
    v_all = jnp.concatenate([vc_ref[0], vp_ref[0], v0_ref[0], vn_ref[0]], axis=0).astype(BF16)
    nk = ctx_len + 3 * blk
    col = lax.broadcasted_iota(jnp.int32, (blk, nk), 1)
    trow = jb * blk + lax.broadcasted_iota(jnp.int32, (blk, nk), 0)
    srow = (jb - 1) * blk + (col - ctx_len)
    band = ((trow >= ctx_len) & (srow >= ctx_len) & (srow < seq) & (jnp.abs(trow - srow) <= WINDOW))
    mask = (col < ctx_len) | band
    group = SW_Q_HEADS // SW_KV_HEADS
    outs = []
    for hq in range(SW_Q_HEADS):
        g = hq // group
        qh = q[:, hq * SW_HEAD_DIM:(hq + 1) * SW_HEAD_DIM]
        kh = k_all[:, g * SW_HEAD_DIM:(g + 1) * SW_HEAD_DIM]
        vh = v_all[:, g * SW_HEAD_DIM:(g + 1) * SW_HEAD_DIM]
        s = lax.dot_general(qh, kh, _NT, preferred_element_type=F32)
        s = jnp.where(mask, s, NEG_INF)
        sk = sink_ref[hq]
        m = jnp.maximum(jnp.max(s, axis=-1, keepdims=True), sk)
        e = jnp.exp(s - m)
        den = jnp.sum(e, axis=-1, keepdims=True) + jnp.exp(sk - m)
        outs.append(jnp.dot(e.astype(BF16), vh, preferred_element_type=F32) / den)
    o_ref[0] = jnp.concatenate(outs, axis=1)


def _sw_attention(z, cos, sin, sink, ctx_len):
    b, s, _ = z.shape
    blk = ATT_BLOCK
    nb = s // blk
    qw = SW_Q_HEADS * SW_HEAD_DIM
    qcol, kcol, vcol = 5 * HG_WIDTH // qw, 5 * HG_WIDTH // LANES + 4, 5 * HG_WIDTH // LANES + 5
    prev = lambda j: jnp.maximum(j - 1, 0)
    nxt = lambda j: jnp.minimum(j + 1, nb - 1)
    kv = lambda col, f: pl.BlockSpec((1, blk, LANES), lambda bi, j: (bi, f(j), col))
    tab = lambda f: pl.BlockSpec((blk, LANES), lambda bi, j: (f(j), 0))
    same = lambda j: j
    return pl.pallas_call(
        functools.partial(_sw_kernel, ctx_len=ctx_len, seq=s),
        grid=(b, nb),
        in_specs=[
            pl.BlockSpec((1, blk, qw), lambda bi, j: (bi, j, qcol)),
            pl.BlockSpec((1, ctx_len, LANES), lambda bi, j: (bi, 0, kcol)),
            pl.BlockSpec((1, ctx_len, LANES), lambda bi, j: (bi, 0, vcol)),
            kv(kcol, prev), kv(kcol, same), kv(kcol, nxt),
            kv(vcol, prev), kv(vcol, same), kv(vcol, nxt),
            tab(same), tab(same), tab(prev), tab(prev), tab(nxt), tab(nxt),
            pl.BlockSpec(memory_space=pltpu.SMEM),
        ],
        out_specs=pl.BlockSpec((1, blk, qw), lambda bi, j: (bi, j, 0)),
        out_shape=jax.ShapeDtypeStruct((b, s, qw), F32),
        compiler_params=_cparams("parallel", "parallel"),
    )(z, z, z, z, z, z, z, z, z, cos, sin, cos, sin, cos, sin, sink)


def _ab_out_kernel(of_ref, ob_ref, g_ref, sw_ref, h_ref, m_ref, ng_ref, w_ref, o_ref):
    o = of_ref[...] + ob_ref[...]
    ng = ng_ref[...]
    parts = []
    for hh in range(HG_HEADS):
        oh = o[:, hh * HG_HEAD_DIM:(hh + 1) * HG_HEAD_DIM]
        ms = jnp.mean(oh * oh, axis=-1, keepdims=True)
        parts.append(oh * lax.rsqrt(ms + EPS) * ng)
    hg = jnp.concatenate(parts, axis=1) * _silu(g_ref[0])
    mix = jnp.concatenate([hg, sw_ref[0]], axis=1).astype(BF16)
    y = jnp.dot(mix, w_ref[...], preferred_element_type=F32)
    o_ref[0] = h_ref[0] + m_ref[0, 0][2:3] * y


def _ab_out(o_f, o_b, z, sw, h, mod, ng, w, ctx_blocks):
    b, s, d = h.shape
    tm = ROW_BLOCK
    sel = lambda i: jnp.minimum(i // ctx_blocks, 1)
    return pl.pallas_call(
        _ab_out_kernel,
        grid=(b, s // tm),
        in_specs=[
            pl.BlockSpec((None, tm, HG_WIDTH), lambda bi, i: (bi, i, 0)),
            pl.BlockSpec((None, tm, HG_WIDTH), lambda bi, i: (bi, i, 0)),
            pl.BlockSpec((1, tm, HG_WIDTH), lambda bi, i: (bi, i, 4)),
            pl.BlockSpec((1, tm, sw.shape[-1]), lambda bi, i: (bi, i, 0)),
            pl.BlockSpec((1, tm, d), lambda bi, i: (bi, i, 0)),
            pl.BlockSpec((1, 1, 8, d), lambda bi, i: (bi, sel(i), 0, 0)),
            pl.BlockSpec((1, HG_HEAD_DIM), lambda bi, i: (0, 0)),
            pl.BlockSpec(w.shape, lambda bi, i: (0, 0)),
        ],
        out_specs=pl.BlockSpec((1, tm, d), lambda bi, i: (bi, i, 0)),
        out_shape=jax.ShapeDtypeStruct((b, s, d), F32),
        compiler_params=_cparams("parallel", "parallel"),
    )(o_f, o_b, z, sw, h, mod, ng.reshape(1, HG_HEAD_DIM), w)


def _ga_prep_kernel(z_ref, cos_ref, sin_ref, qg_ref, kg_ref, q_ref, k_ref, v_ref):
    z = z_ref[0]
    cos, sin = cos_ref[...], sin_ref[...]
    hd = GA_HEAD_DIM

    def norm_rope(x, g):
        ms = jnp.mean(x * x, axis=-1, keepdims=True)
        return _rope_half(x * lax.rsqrt(ms + EPS) * g, cos, sin, hd // 2)

    qg, kg = qg_ref[...], kg_ref[...]
    qs = [norm_rope(z[:, i * hd:(i + 1) * hd], qg) * (hd ** -0.5) for i in range(GA_Q_HEADS)]
    q_ref[0] = jnp.concatenate(qs, axis=1).astype(BF16)
    k0 = GA_Q_HEADS * hd
    ks = [norm_rope(z[:, k0 + i * hd:k0 + (i + 1) * hd], kg) for i in range(GA_KV_HEADS)]
    k_ref[0] = jnp.concatenate(ks, axis=1).astype(BF16)
    v_ref[0] = z[:, k0 + GA_KV_HEADS * hd:].astype(BF16)


def _ga_prep(z, cos, sin, qg, kg):
    b, s, n = z.shape
    tm = ROW_BLOCK
    qw, kw = GA_Q_HEADS * GA_HEAD_DIM, GA_KV_HEADS * GA_HEAD_DIM
    row = lambda width: pl.BlockSpec((1, tm, width), lambda bi, i: (bi, i, 0))
    return pl.pallas_call(
        _ga_prep_kernel,
        grid=(b, s // tm),
        in_specs=[
            row(n),
            pl.BlockSpec((tm, GA_HEAD_DIM), lambda bi, i: (i, 0)),
            pl.BlockSpec((tm, GA_HEAD_DIM), lambda bi, i: (i, 0)),
            pl.BlockSpec((1, GA_HEAD_DIM), lambda bi, i: (0, 0)),
            pl.BlockSpec((1, GA_HEAD_DIM), lambda bi, i: (0, 0)),
        ],
        out_specs=[row(qw), row(kw), row(kw)],
        out_shape=[jax.ShapeDtypeStruct((b, s, qw), BF16),
                   jax.ShapeDtypeStruct((b, s, kw), BF16),
                   jax.ShapeDtypeStruct((b, s, kw), BF16)],
        compiler_params=_cparams("parallel", "parallel"),
    )(z, cos, sin, qg.reshape(1, -1), kg.reshape(1, -1))


_GA_TQ = 256
_GA_CHUNKS = 2


def _flash_kernel(q_ref, k_ref, v_ref, o_ref, s_ref):
    hd = GA_HEAD_DIM
    group = GA_Q_HEADS // GA_KV_HEADS
    q = q_ref[0]
    tq = q.shape[0]
    qs = jnp.concatenate([q[:, i * hd:(i + 1) * hd] for i in range(group)], axis=0)
    nk = k_ref.shape[1]
    kc = nk // _GA_CHUNKS
    for c in range(_GA_CHUNKS):
        s_ref[:, c * kc:(c + 1) * kc] = lax.dot_general(qs, k_ref[0, c * kc:(c + 1) * kc, :], _NT,
                                                        preferred_element_type=F32)
    tile = lambda rows, j: s_ref[rows, j * LANES:(j + 1) * LANES]
    outs = []
    for g in range(group):
        rows = slice(g * tq, (g + 1) * tq)
        m = tile(rows, 0)
        for j in range(1, nk // LANES):
            m = jnp.maximum(m, tile(rows, j))
        m = jnp.broadcast_to(jnp.max(m, axis=-1, keepdims=True), m.shape)
        lsum = jnp.zeros_like(m)
        ps = []
        for j in range(nk // LANES):
            pj = jnp.exp(tile(rows, j) - m)
            lsum = lsum + pj
            ps.append(pj.astype(BF16))
        o = jnp.dot(jnp.concatenate(ps, axis=1), v_ref[0], preferred_element_type=F32)
        outs.append(o / jnp.sum(lsum, axis=-1, keepdims=True))
    o_ref[0] = jnp.concatenate(outs, axis=1).astype(o_ref.dtype)


def _ga_attention(q, k, v, ctx_len):
    b, s, qw = q.shape
    l = s - ctx_len
    tq = _GA_TQ
    group = GA_Q_HEADS // GA_KV_HEADS
    gw = group * GA_HEAD_DIM
    off = ctx_len // tq
    return pl.pallas_call(
        _flash_kernel,
        grid=(b, GA_KV_HEADS, l // tq),
        in_specs=[
            pl.BlockSpec((1, tq, gw), lambda bi, g, i: (bi, off + i, g)),
            pl.BlockSpec((1, s, GA_HEAD_DIM), lambda bi, g, i: (bi, 0, g)),
            pl.BlockSpec((1, s, GA_HEAD_DIM), lambda bi, g, i: (bi, 0, g)),
        ],
        out_specs=pl.BlockSpec((1, tq, gw), lambda bi, g, i: (bi, i, g)),
        out_shape=jax.ShapeDtypeStruct((b, l, qw), BF16),
        scratch_shapes=[pltpu.VMEM((group * tq, s), F32)],
        compiler_params=_cparams("parallel", "parallel", "parallel"),
    )(q, k, v)


def _out_res_kernel(a_ref, h_ref, m_ref, w_ref, o_ref):
    y = jnp.dot(a_ref[0], w_ref[...], preferred_element_type=F32)
    o_ref[0] = h_ref[0] + m_ref[0, 0][2:3] * y


def _out_res(a, h, mod, w, ctx_len):
    b, l, n = a.shape
    d = h.shape[-1]
    tm = ROW_BLOCK
    off = ctx_len // tm
    return pl.pallas_call(
        _out_res_kernel,
        grid=(b, l // tm),
        in_specs=[
            pl.BlockSpec((1, tm, n), lambda bi, i: (bi, i, 0)),
            pl.BlockSpec((1, tm, d), lambda bi, i: (bi, off + i, 0)),
            pl.BlockSpec((1, 1, 8, d), lambda bi, i: (bi, 1, 0, 0)),
            pl.BlockSpec(w.shape, lambda bi, i: (0, 0)),
        ],
        out_specs=pl.BlockSpec((1, tm, d), lambda bi, i: (bi, i, 0)),
        out_shape=jax.ShapeDtypeStruct((b, l, d), F32),
        compiler_params=_cparams("parallel", "parallel"),
    )(a, h, mod, w)


_NO_RANK = 1 << 20


def _topk_rows(s, rank=None, payload=None):
    if rank is None:
        rank = lax.broadcasted_iota(jnp.int32, s.shape, 0)
    vals, sel = [], []
    for _ in range(PEER_TOPK):
        m = jnp.max(s, axis=0, keepdims=True)
        i = jnp.min(jnp.where(s == m, rank, _NO_RANK), axis=0, keepdims=True)
        hit = rank == i
        vals.append(m)
        sel.append(i if payload is None else jnp.max(jnp.where(hit, payload, -1), axis=0, keepdims=True))
        s = jnp.where(hit, -jnp.inf, s)
    return jnp.concatenate(vals, axis=0), jnp.concatenate(sel, axis=0)


def _cand_slabs():
    k = PEER_TOPK
    slabs = [('b', 0, 0), ('b', 0, 8)] + [('b', a, 0) for a in range(1, 8)] + [('a', 8, 0)]
    rank = []
    for kind, a, b in slabs:
        for j in range(SUBLANES):
            aa, bb = (a, b + j) if kind == 'b' else (a + j, b)
            rank.append(aa * k + bb if (aa + 1) * (bb + 1) <= k else _NO_RANK)
    return slabs, np.asarray(rank, np.int32)


def _route_kernel(q_ref, keys_ref, rank_ref, ids_ref, gate_ref):
    q = q_ref[0]
    slabs, _ = _cand_slabs()
    rank = jnp.concatenate([rank_ref[...]] * (q.shape[0] // LANES), axis=1)
    ids, gates = [], []
    for h in range(PEER_HEADS):
        tops = []
        for p in range(2):
            c0 = (h * 2 + p) * PEER_HALF
            st = lax.dot_general(keys_ref[h, p], q[:, c0:c0 + PEER_HALF], _NT,
                                 precision=HIGHEST, preferred_element_type=F32)
            tops.append(_topk_rows(st))
        (s1, i1), (s2, i2) = tops
        pick = lambda x, y, kind, a, b: ((x[a:a + 1], y[b:b + SUBLANES]) if kind == 'b'
                                         else (x[a:a + SUBLANES], y[b:b + 1]))
        cand = jnp.concatenate([sum(pick(s1, s2, *sl)) for sl in slabs], axis=0)
        cand = jnp.where(rank < _NO_RANK, cand, -jnp.inf)
        eid = jnp.concatenate([(lambda x, y: x * PEER_N_KEYS + y)(*pick(i1, i2, *sl)) for sl in slabs],
                              axis=0)
        sc, e = _topk_rows(cand, rank, eid)
        ex = jnp.exp(sc - jnp.max(sc, axis=0, keepdims=True))
        gates.append(ex / jnp.sum(ex, axis=0, keepdims=True))
        ids.append(e)
    ids_ref[0] = jnp.concatenate(ids, axis=0).astype(F32).T.astype(jnp.int32)
    gate_ref[0] = jnp.concatenate(gates, axis=0).T


def _route(q, keys):
    b, r, n = q.shape
    tm = ROW_BLOCK
    rank = np.tile(_cand_slabs()[1][:, None], (1, LANES))
    return pl.pallas_call(
        _route_kernel,
        grid=(b, r // tm),
        in_specs=[
            pl.BlockSpec((1, tm, n), lambda bi, i: (bi, i, 0)),
            pl.BlockSpec(keys.shape, lambda bi, i: (0, 0, 0, 0)),
            pl.BlockSpec(rank.shape, lambda bi, i: (0, 0)),
        ],
        out_specs=[pl.BlockSpec((1, tm, PEER_PAIRS), lambda bi, i: (bi, i, 0)),
                   pl.BlockSpec((1, tm, PEER_PAIRS), lambda bi, i: (bi, i, 0))],
        out_shape=[jax.ShapeDtypeStruct((b, r, PEER_PAIRS), jnp.int32),
                   jax.ShapeDtypeStruct((b, r, PEER_PAIRS), F32)],
        compiler_params=_cparams("parallel", "parallel"),
    )(q, keys, jnp.asarray(rank))


_EX_TOK = 8
_EX_HALVES = 4
_EX_AHEAD = 2
_EX_NP = _EX_TOK * PEER_PAIRS
_ROW_TILES = 8
_ROW_PITCH = 12
_EX_GROUPS = 2 * _ROW_TILES


def _expert_kernel(ids_ref, idsn_ref, f_ref, gate_ref, h_ref, m_ref, fg_ref, ex_ref, uv_ref, o_ref,
                   *scratch, n_steps, final):
    bufs, sem = scratch[:_EX_HALVES], scratch[_EX_HALVES]
    step = pl.program_id(0) * pl.num_programs(1) + pl.program_id(1)
    per_group = _EX_NP // _EX_GROUPS

    def start_copy(e, dst, p, sem_i, priority):
        pltpu.make_async_copy(uv_ref.at[e], dst.at[pl.ds(p * _ROW_PITCH, _ROW_TILES), :],
                              sem.at[sem_i]).start(priority=priority)

    def wait_all(dst, sem_i):
        rows = pl.ds(0, _EX_NP * _ROW_TILES)
        pltpu.make_async_copy(dst.at[rows, :], dst.at[rows, :], sem.at[sem_i]).wait()

    def half(cur, half_i, issue_ids, issue_off, other, other_sem):
        rows = slice(half_i * _EX_TOK, (half_i + 1) * _EX_TOK)
        f = f_ref[0, rows, :]

        def issue(g):
            for u in range(per_group):
                p = g * per_group + u
                start_copy(issue_ids[0, issue_off + p], other, p, other_sem, u % 2)

        words = lambda r: pltpu.bitcast(cur[pl.ds(r, _EX_NP, stride=_ROW_PITCH), :], BF16)
        a = jnp.zeros((_EX_TOK, _EX_NP), F32)
        for r in range(_ROW_TILES):
            issue(r)
            u = pltpu.bitcast(lax.shift_left(cur[pl.ds(r, _EX_NP, stride=_ROW_PITCH), :], 16), F32)
            a = a + lax.dot_general(f[:, r * LANES:(r + 1) * LANES], u.astype(BF16), _NT,
                                    preferred_element_type=F32)
        act = 0.5 * a * (1.0 + lax.erf(a * (2.0 ** -0.5)))
        gate = jnp.concatenate([gate_ref[0, rows, :]] * _EX_TOK, axis=1)
        own = (lax.broadcasted_iota(jnp.int32, a.shape, 1) // PEER_PAIRS
               == lax.broadcasted_iota(jnp.int32, a.shape, 0))
        ga = jnp.where(own, gate * act, 0.0).astype(BF16)
        coef = jnp.concatenate(
            [jnp.dot(ga[:, t * PEER_PAIRS:(t + 1) * PEER_PAIRS], ex_ref[...], preferred_element_type=F32)
             for t in range(_EX_TOK)], axis=1).astype(BF16)
        ys = []
        for r in range(_ROW_TILES):
            issue(_ROW_TILES + r)
            ys.append(jnp.dot(coef, words(r), preferred_element_type=F32))
        hn = h_ref[0, rows, :] + m_ref[0, 0][5:6] * jnp.concatenate(ys, axis=1)
        if final:
            ms = jnp.mean(hn * hn, axis=-1, keepdims=True)
            hn = hn * lax.rsqrt(ms + EPS) * fg_ref[...]
        o_ref[0, rows, :] = hn

    @pl.when(step == 0)
    def _():
        for k in range(_EX_AHEAD):
            def body(p, _, k=k):
                start_copy(ids_ref[0, k * _EX_NP + p], bufs[k], p, k, 0)
                return 0
            lax.fori_loop(0, _EX_NP, body, 0)

    for k in range(_EX_HALVES):
        wait_all(bufs[k], k)
        tgt = k + _EX_AHEAD
        if tgt < _EX_HALVES:
            half(bufs[k], k, ids_ref, tgt * _EX_NP, bufs[tgt], tgt)
        else:
            half(bufs[k], k, idsn_ref, (tgt - _EX_HALVES) * _EX_NP, bufs[tgt - _EX_HALVES],
                 tgt - _EX_HALVES)

    @pl.when(step == n_steps - 1)
    def _():
        for k in range(_EX_AHEAD):
            wait_all(bufs[k], k)


def _experts(ids, gates, f, h, mod, uv, final_g, *, ctx_blocks, final):
    b, r, d = h.shape
    tok = _EX_HALVES * _EX_TOK
    nb = r // tok
    n_steps = b * nb
    ids_flat = ids.reshape(n_steps, 1, tok * PEER_PAIRS)
    per_ctx = ctx_blocks * (ROW_BLOCK // tok)
    sel = (lambda i: jnp.minimum(i // per_ctx, 1)) if ctx_blocks else (lambda i: 1)
    blk = lambda width: pl.BlockSpec((1, tok, width), lambda bi, i: (bi, i, 0))
    lin = lambda bi, i: bi * nb + i
    expand = np.zeros((PEER_PAIRS, 2 * PEER_PAIRS), np.float32)
    expand[np.arange(PEER_PAIRS), 2 * np.arange(PEER_PAIRS) + 1] = 1.0
    buf = pltpu.VMEM((_EX_NP * _ROW_PITCH, LANES), jnp.int32)
    return pl.pallas_call(
        functools.partial(_expert_kernel, n_steps=n_steps, final=final),
        grid=(b, nb),
        in_specs=[
            pl.BlockSpec((None, 1, tok * PEER_PAIRS), lambda bi, i: (lin(bi, i), 0, 0),
                         memory_space=pltpu.SMEM),
            pl.BlockSpec((None, 1, tok * PEER_PAIRS),
                         lambda bi, i: (jnp.minimum(lin(bi, i) + 1, n_steps - 1), 0, 0),
                         memory_space=pltpu.SMEM),
            blk(d), blk(PEER_PAIRS), blk(d),
            pl.BlockSpec((1, 1, 8, d), lambda bi, i: (bi, sel(i), 0, 0)),
            pl.BlockSpec((1, d), lambda bi, i: (0, 0)),
            pl.BlockSpec(expand.shape, lambda bi, i: (0, 0)),
            pl.BlockSpec(memory_space=pl.ANY),
        ],
        out_specs=blk(d),
        out_shape=jax.ShapeDtypeStruct((b, r, d), F32),
        scratch_shapes=[buf] * _EX_HALVES + [pltpu.SemaphoreType.DMA((_EX_HALVES,))],
        compiler_params=_cparams("arbitrary", "arbitrary"),
    )(ids_flat, ids_flat, f, gates, h, mod, final_g.reshape(1, d), jnp.asarray(expand, BF16), uv)


def _pack_experts(u, v):
    ub = lax.bitcast_convert_type(u.astype(BF16), jnp.uint16).astype(jnp.uint32)
    vb = lax.bitcast_convert_type(v.astype(BF16), jnp.uint16).astype(jnp.uint32)
    words = lax.bitcast_convert_type(ub | (vb << 16), jnp.int32)
    return words.reshape(u.shape[0], _ROW_TILES, LANES)


def _rope_tables(seq, ctx_len, head_dim):
    n_rows = seq // GRID_W
    row = jnp.repeat(jnp.arange(n_rows), GRID_W).astype(F32)
    col = jnp.tile(jnp.arange(GRID_W), n_rows).astype(F32)
    n_freq = head_dim // 4
    inv = ROPE_THETA ** (-jnp.arange(n_freq, dtype=F32) / n_freq)
    ang = jnp.concatenate([row[:, None] * inv, col[:, None] * inv], axis=-1)
    cos = jnp.concatenate([jnp.cos(ang), jnp.cos(ang)], axis=-1)
    sin = jnp.concatenate([-jnp.sin(ang), jnp.sin(ang)], axis=-1)
    reps = LANES // head_dim
    cos, sin = jnp.tile(cos, (1, reps)), jnp.tile(sin, (1, reps))
    cos = jnp.concatenate([jnp.ones((ctx_len, LANES), F32), cos], axis=0)
    sin = jnp.concatenate([jnp.zeros((ctx_len, LANES), F32), sin], axis=0)
    return cos, sin


def kernel(x, c, ctx, c_ctx, ada_w, ada_b, norm_mix_g, norm_ffn_g, ab_w_in, ab_w_out, hg_lb_logits,
           hg_norm_g, sw_sink, ga_w_in, ga_w_out, ga_q_norm_g, ga_k_norm_g, peer_w_q, peer_sub_keys,
           peer_u, peer_v, final_norm_g):
    b, l, d = x.shape
    lc = ctx.shape[1]
    assert lc % ROW_BLOCK == 0 and l % ROW_BLOCK == 0 and ada_w.shape[0] == 2
    ctx_blocks = lc // ROW_BLOCK
    h = jnp.concatenate([ctx, x], axis=1)

    n_mod = -(-(b + 1) // SUBLANES) * SUBLANES
    cc = jnp.zeros((n_mod, d), F32).at[:b].set(c).at[b].set(c_ctx)
    mod_all = _ada(cc, ada_w, ada_b)

    def mod_for(layer):
        lat = mod_all[layer, :b].reshape(b, 1, 6, d)
        cx = jnp.broadcast_to(mod_all[layer, b].reshape(1, 1, 6, d), (b, 1, 6, d))
        m = jnp.concatenate([cx, lat], axis=1)
        return jnp.pad(m, ((0, 0), (0, 0), (0, 2), (0, 0)))

    lbs = jnp.cumsum(jax.nn.softmax(hg_lb_logits.astype(F32), axis=0), axis=0)

    m0 = mod_for(0)
    z = _nmm(h, m0, norm_mix_g[0], ab_w_in[0].astype(BF16), row0=0, ctx_blocks=ctx_blocks)
    o_f, o_b = _hgrn(z, lbs[0], lc)
    cos_sw, sin_sw = _rope_tables(l, lc, SW_HEAD_DIM)
    sw = _sw_attention(z, cos_sw, sin_sw, sw_sink[0], lc)
    h = _ab_out(o_f, o_b, z, sw, h, m0, hg_norm_g[0], ab_w_out[0].astype(BF16), ctx_blocks)
    q, f = _nmm(h, m0, norm_ffn_g[0], peer_w_q[0].astype(BF16), row0=3, ctx_blocks=ctx_blocks,
                want_f=True)
    ids, gates = _route(q, peer_sub_keys[0])
    h = _experts(ids, gates, f, h, m0, _pack_experts(peer_u[0], peer_v[0]), final_norm_g,
                 ctx_blocks=ctx_blocks, final=False)

    m1 = mod_for(1)
    z = _nmm(h, m1, norm_mix_g[1], ga_w_in[0].astype(BF16), row0=0, ctx_blocks=ctx_blocks)
    cos_ga, sin_ga = _rope_tables(l, lc, GA_HEAD_DIM)
    qg, kg, vg = _ga_prep(z, cos_ga, sin_ga, ga_q_norm_g[0], ga_k_norm_g[0])
    att = _ga_attention(qg, kg, vg, lc)
    hl = _out_res(att, h, m1, ga_w_out[0].astype(BF16), lc)
    q, f = _nmm(hl, m1, norm_ffn_g[1], peer_w_q[1].astype(BF16), row0=3, ctx_blocks=0, want_f=True)
    ids, gates = _route(q, peer_sub_keys[1])
    return _experts(ids, gates, f, hl, m1, _pack_experts(peer_u[1], peer_v[1]), final_norm_g,
                    ctx_blocks=0, final=True)
```

```python
import functools

import numpy as np
import jax
import jax.numpy as jnp
from jax import lax
from jax.experimental import pallas as pl
from jax.experimental.pallas import tpu as pltpu

F32 = jnp.float32
BF16 = jnp.bfloat16
HIGHEST = lax.Precision.HIGHEST

EPS = 1e-6
NEG_INF = -1e30
ROPE_THETA = 10000.0
GRID_W = 64

HG_HEAD_DIM = 128
HG_HEADS = 4
HG_WIDTH = HG_HEADS * HG_HEAD_DIM
HG_CHUNK = 64
SW_HEAD_DIM = 64
SW_Q_HEADS = 8
SW_KV_HEADS = 2
WINDOW = 128
ATT_BLOCK = 128
GA_HEAD_DIM = 128
GA_Q_HEADS = 8
GA_KV_HEADS = 2
PEER_HEADS = 8
PEER_N_KEYS = 128
PEER_HALF = 128
PEER_TOPK = 16
PEER_PAIRS = PEER_HEADS * PEER_TOPK

LANES = 128
SUBLANES = 8
ROW_BLOCK = 256
VMEM_LIMIT = 56 * 1024 * 1024

_NT = (((1,), (1,)), ((), ()))
_TN = (((0,), (0,)), ((), ()))


def _cparams(*sem):
    return pltpu.CompilerParams(dimension_semantics=sem, vmem_limit_bytes=VMEM_LIMIT)


def _sigmoid(x):
    return 1.0 / (1.0 + jnp.exp(-x))


def _silu(x):
    return x * _sigmoid(x)


def _ada_kernel(c_ref, w_ref, b_ref, o_ref):
    a = _silu(c_ref[...])
    o_ref[0] = jnp.dot(a, w_ref[0], precision=HIGHEST, preferred_element_type=F32) + b_ref[0]


def _ada(cc, ada_w, ada_b):
    depth, d, n = ada_w.shape
    rows = cc.shape[0]
    tn = 1536
    return pl.pallas_call(
        _ada_kernel,
        grid=(depth, n // tn),
        in_specs=[
            pl.BlockSpec((rows, d), lambda l, j: (0, 0)),
            pl.BlockSpec((1, d, tn), lambda l, j: (l, 0, j)),
            pl.BlockSpec((1, 1, tn), lambda l, j: (l, 0, j)),
        ],
        out_specs=pl.BlockSpec((1, rows, tn), lambda l, j: (l, 0, j)),
        out_shape=jax.ShapeDtypeStruct((depth, rows, n), F32),
        compiler_params=_cparams("parallel", "parallel"),
    )(cc, ada_w, ada_b.reshape(depth, 1, n))


def _nmm_kernel(h_ref, m_ref, g_ref, w_ref, o_ref, *f_ref, row0):
    x = h_ref[0]
    ms = jnp.mean(x * x, axis=-1, keepdims=True)
    y = x * lax.rsqrt(ms + EPS) * g_ref[...]
    m = m_ref[0, 0]
    a = (y * (1.0 + m[row0 + 1:row0 + 2]) + m[row0:row0 + 1]).astype(BF16)
    if f_ref:
        f_ref[0][0] = a
    o_ref[0] = jnp.dot(a, w_ref[...], preferred_element_type=F32)


def _nmm(h, mod, g, w, *, row0, ctx_blocks, want_f=False):
    b, r, d = h.shape
    n = w.shape[1]
    tm = ROW_BLOCK
    sel = (lambda i: jnp.minimum(i // ctx_blocks, 1)) if ctx_blocks else (lambda i: 1)
    out_shape = [jax.ShapeDtypeStruct((b, r, n), F32)]
    out_specs = [pl.BlockSpec((1, tm, n), lambda bi, i: (bi, i, 0))]
    if want_f:
        out_shape.append(jax.ShapeDtypeStruct((b, r, d), BF16))
        out_specs.append(pl.BlockSpec((1, tm, d), lambda bi, i: (bi, i, 0)))
    res = pl.pallas_call(
        functools.partial(_nmm_kernel, row0=row0),
        grid=(b, r // tm),
        in_specs=[
            pl.BlockSpec((1, tm, d), lambda bi, i: (bi, i, 0)),
            pl.BlockSpec((1, 1, 8, d), lambda bi, i: (bi, sel(i), 0, 0)),
            pl.BlockSpec((1, d), lambda bi, i: (0, 0)),
            pl.BlockSpec((d, n), lambda bi, i: (0, 0)),
        ],
        out_specs=out_specs,
        out_shape=out_shape,
        compiler_params=_cparams("parallel", "parallel"),
    )(h, mod, g.reshape(1, d), w)
    return res if want_f else res[0]


_HG_LEVELS = (32, 16, 8, 4, 2, 1)
_HG_STACK_ROWS = HG_CHUNK * (2 + 2 * len(_HG_LEVELS)) + SUBLANES


def _hg_tables():
    c = HG_CHUNK
    stacks, masks = [], []
    for rev in (0, 1):
        pos = np.arange(c) if rev == 0 else c - 1 - np.arange(c)
        tt, ss = pos[:, None], pos[None, :]
        rows = [(ss <= tt)]
        qrows, krows, lmask = [], [], []
        for m in _HG_LEVELS:
            same = (tt // (2 * m)) == (ss // (2 * m))
            pt, ps = tt % (2 * m), ss % (2 * m)
            qrows.append(same & (pt >= m) & (ps >= m) & (ps <= pt))
            krows.append(same & (pt < m) & (ps > pt) & (ps <= m - 1))
            lmask.append(same & (pt >= m) & (ps < m))
        lmask.append(tt == ss)
        rows += qrows + krows + [(ss > tt), np.ones((SUBLANES, c), bool)]
        stacks.append(np.tile(np.concatenate(rows, axis=0).astype(np.float32), (1, 3)))
        masks.append(np.stack(lmask).astype(np.float32))
    return np.stack(stacks), np.stack(masks)


def _hg_kernel(qf_ref, ff_ref, if_ref, qb_ref, fb_ref, ib_ref, lb_ref, w_ref, ml_ref, of_ref, ob_ref,
               st_ref, *, n_chunks):
    t = pl.program_id(1)

    @pl.when(t == 0)
    def _():
        st_ref[...] = jnp.zeros_like(st_ref)

    def chunk(ci, _):
        _hg_chunk(0, ci, qf_ref, ff_ref, if_ref, lb_ref, w_ref, ml_ref, of_ref, st_ref)
        _hg_chunk(1, n_chunks - 1 - ci, qb_ref, fb_ref, ib_ref, lb_ref, w_ref, ml_ref, ob_ref, st_ref)
        return 0

    lax.fori_loop(0, n_chunks, chunk, 0)


def _hg_chunk(d, cc, q_ref, f_ref, i_ref, lb_ref, w_ref, ml_ref, o_ref, st_ref):
    c = HG_CHUNK
    nl = len(_HG_LEVELS)
    lb = lb_ref[...]
    w = w_ref[d]
    rows = pl.ds(pl.multiple_of(cc * c, c), c)
    x = f_ref[rows, :]
    k = (1.0 - lb) * _sigmoid(-x)
    lf = jnp.log(1.0 - k)
    q = _silu(q_ref[rows, :])
    v = i_ref[rows, :]
    hi = lf.astype(BF16)
    r1 = lf - hi.astype(F32)
    mid = r1.astype(BF16)
    lo = (r1 - mid.astype(F32)).astype(BF16)
    e = jnp.exp(jnp.dot(w, jnp.concatenate([hi, mid, lo], axis=0), preferred_element_type=F32))
    qe = q * e[0:c]
    kend = k * e[c * (1 + 2 * nl):c * (2 + 2 * nl)]
    tot = e[c * (2 + 2 * nl):c * (2 + 2 * nl) + 1]
    scaled = [(q * e[c * (1 + l):c * (2 + l)], k * e[c * (1 + nl + l):c * (2 + nl + l)])
              for l in range(nl)]
    outs = []
    for hh in range(HG_HEADS):
        hs = slice(hh * HG_HEAD_DIM, (hh + 1) * HG_HEAD_DIM)
        st = st_ref[d, hh]
        o = lax.dot_general(qe[:, hs], st, _NT, preferred_element_type=F32)
        a = ml_ref[d, nl] * lax.dot_general(q[:, hs], k[:, hs], _NT, preferred_element_type=F32)
        for l in range(nl):
            a = a + ml_ref[d, l] * lax.dot_general(scaled[l][0][:, hs], scaled[l][1][:, hs], _NT,
                                                   preferred_element_type=F32)
        outs.append(o + jnp.dot(a, v[:, hs], preferred_element_type=F32))
        st_ref[d, hh] = st * tot[:, hs] + lax.dot_general(v[:, hs], kend[:, hs], _TN,
                                                          preferred_element_type=F32)
    o_ref[rows, :] = jnp.concatenate(outs, axis=1)


def _hgrn(z, lb, ctx_len):
    b, s, _ = z.shape
    tb = ROW_BLOCK
    nt, nctx = s // tb, ctx_len // tb
    stack, masks = _hg_tables()

    bwd = lambda t: jnp.where(t < nctx, nctx - 1 - t, nt - 1 - (t - nctx))
    fcol = lambda col: pl.BlockSpec((None, tb, HG_WIDTH), lambda bi, t: (bi, t, col))
    bcol = lambda col: pl.BlockSpec((None, tb, HG_WIDTH), lambda bi, t: (bi, bwd(t), col))
    return pl.pallas_call(
        functools.partial(_hg_kernel, n_chunks=tb // HG_CHUNK),
        grid=(b, nt),
        in_specs=[
            fcol(0), fcol(1), fcol(3), bcol(0), bcol(2), bcol(3),
            pl.BlockSpec((1, HG_WIDTH), lambda bi, t: (0, 0)),
            pl.BlockSpec(stack.shape, lambda bi, t: (0, 0, 0)),
            pl.BlockSpec(masks.shape, lambda bi, t: (0, 0, 0, 0)),
        ],
        out_specs=[fcol(0), bcol(0)],
        out_shape=[jax.ShapeDtypeStruct((b, s, HG_WIDTH), F32)] * 2,
        scratch_shapes=[pltpu.VMEM((2, HG_HEADS, HG_HEAD_DIM, HG_HEAD_DIM), F32)],
        compiler_params=_cparams("parallel", "arbitrary"),
    )(z, z, z, z, z, z, lb.reshape(1, HG_WIDTH), jnp.asarray(stack, BF16), jnp.asarray(masks))


def _rope_half(x, cos, sin_signed, half):
    width = x.shape[-1]
    if 2 * half == width:
        sw = pltpu.roll(x, half, 1)
    else:
        lane = lax.broadcasted_iota(jnp.int32, x.shape, 1)
        sw = jnp.where(lane % (2 * half) < half, pltpu.roll(x, width - half, 1), pltpu.roll(x, half, 1))
    return x * cos + sw * sin_signed


def _sw_kernel(q_ref, kc_ref, vc_ref, kp_ref, k0_ref, kn_ref, vp_ref, v0_ref, vn_ref,
               cq_ref, sq_ref, cp_ref, sp_ref, cn_ref, sn_ref, sink_ref, o_ref, *, ctx_len, seq):
    jb = pl.program_id(1)
    blk = ATT_BLOCK
    half = SW_HEAD_DIM // 2
    cq, sq = cq_ref[...], sq_ref[...]
    reps = q_ref.shape[-1] // LANES
    q = (_rope_half(q_ref[0], jnp.concatenate([cq] * reps, axis=1),
                    jnp.concatenate([sq] * reps, axis=1), half) * (SW_HEAD_DIM ** -0.5)).astype(BF16)
    kw = jnp.concatenate([kp_ref[0], k0_ref[0], kn_ref[0]], axis=0)
    cw = jnp.concatenate([cp_ref[...], cq, cn_ref[...]], axis=0)
    sw = jnp.concatenate([sp_ref[...], sq, sn_ref[...]], axis=0)
    kw = _rope_half(kw, cw, sw, half)
    k_all = jnp.concatenate([kc_ref[0], kw], axis=0).astype(BF16)
    v_all = jnp.concatenate([vc_ref[0], vp_ref[0], v0_ref[0], vn_ref[0]], axis=0).astype(BF16)
    nk = ctx_len + 3 * blk
    col = lax.broadcasted_iota(jnp.int32, (blk, nk), 1)
    trow = jb * blk + lax.broadcasted_iota(jnp.int32, (blk, nk), 0)
    srow = (jb - 1) * blk + (col - ctx_len)
    band = ((trow >= ctx_len) & (srow >= ctx_len) & (srow < seq) & (jnp.abs(trow - srow) <= WINDOW))
    mask = (col < ctx_len) | band
    group = SW_Q_HEADS // SW_KV_HEADS
    outs = []
    for hq in range(SW_Q_HEADS):
        g = hq // group
        qh = q[:, hq * SW_HEAD_DIM:(hq + 1) * SW_HEAD_DIM]
        kh = k_all[:, g * SW_HEAD_DIM:(g + 1) * SW_HEAD_DIM]
        vh = v_all[:, g * SW_HEAD_DIM:(g + 1) * SW_HEAD_DIM]
        s = lax.dot_general(qh, kh, _NT, preferred_element_type=F32)
        s = jnp.where(mask, s, NEG_INF)
        sk = sink_ref[hq]
        m = jnp.maximum(jnp.max(s, axis=-1, keepdims=True), sk)
        e = jnp.exp(s - m)
        den = jnp.sum(e, axis=-1, keepdims=True) + jnp.exp(sk - m)
        outs.append(jnp.dot(e.astype(BF16), vh, preferred_element_type=F32) / den)
    o_ref[0] = jnp.concatenate(outs, axis=1)


def _sw_attention(z, cos, sin, sink, ctx_len):
    b, s, _ = z.shape
    blk = ATT_BLOCK
    nb = s // blk
    qw = SW_Q_HEADS * SW_HEAD_DIM
    qcol, kcol, vcol = 5 * HG_WIDTH // qw, 5 * HG_WIDTH // LANES + 4, 5 * HG_WIDTH // LANES + 5
    prev = lambda j: jnp.maximum(j - 1, 0)
    nxt = lambda j: jnp.minimum(j + 1, nb - 1)
    kv = lambda col, f: pl.BlockSpec((1, blk, LANES), lambda bi, j: (bi, f(j), col))
    tab = lambda f: pl.BlockSpec((blk, LANES), lambda bi, j: (f(j), 0))
    same = lambda j: j
    return pl.pallas_call(
        functools.partial(_sw_kernel, ctx_len=ctx_len, seq=s),
        grid=(b, nb),
        in_specs=[
            pl.BlockSpec((1, blk, qw), lambda bi, j: (bi, j, qcol)),
            pl.BlockSpec((1, ctx_len, LANES), lambda bi, j: (bi, 0, kcol)),
            pl.BlockSpec((1, ctx_len, LANES), lambda bi, j: (bi, 0, vcol)),
            kv(kcol, prev), kv(kcol, same), kv(kcol, nxt),
            kv(vcol, prev), kv(vcol, same), kv(vcol, nxt),
            tab(same), tab(same), tab(prev), tab(prev), tab(nxt), tab(nxt),
            pl.BlockSpec(memory_space=pltpu.SMEM),
        ],
        out_specs=pl.BlockSpec((1, blk, qw), lambda bi, j: (bi, j, 0)),
        out_shape=jax.ShapeDtypeStruct((b, s, qw), F32),
        compiler_params=_cparams("parallel", "parallel"),
    )(z, z, z, z, z, z, z, z, z, cos, sin, cos, sin, cos, sin, sink)


def _ab_out_kernel(of_ref, ob_ref, g_ref, sw_ref, h_ref, m_ref, ng_ref, w_ref, o_ref):
    o = of_ref[...] + ob_ref[...]
    ng = ng_ref[...]
    parts = []
    for hh in range(HG_HEADS):
        oh = o[:, hh * HG_HEAD_DIM:(hh + 1) * HG_HEAD_DIM]
        ms = jnp.mean(oh * oh, axis=-1, keepdims=True)
        parts.append(oh * lax.rsqrt(ms + EPS) * ng)
    hg = jnp.concatenate(parts, axis=1) * _silu(g_ref[0])
    mix = jnp.concatenate([hg, sw_ref[0]], axis=1).astype(BF16)
    y = jnp.dot(mix, w_ref[...], preferred_element_type=F32)
    o_ref[0] = h_ref[0] + m_ref[0, 0][2:3] * y


def _ab_out(o_f, o_b, z, sw, h, mod, ng, w, ctx_blocks):
    b, s, d = h.shape
    tm = ROW_BLOCK
    sel = lambda i: jnp.minimum(i // ctx_blocks, 1)
    return pl.pallas_call(
        _ab_out_kernel,
        grid=(b, s // tm),
        in_specs=[
            pl.BlockSpec((None, tm, HG_WIDTH), lambda bi, i: (bi, i, 0)),
            pl.BlockSpec((None, tm, HG_WIDTH), lambda bi, i: (bi, i, 0)),
            pl.BlockSpec((1, tm, HG_WIDTH), lambda bi, i: (bi, i, 4)),
            pl.BlockSpec((1, tm, sw.shape[-1]), lambda bi, i: (bi, i, 0)),
            pl.BlockSpec((1, tm, d), lambda bi, i: (bi, i, 0)),
            pl.BlockSpec((1, 1, 8, d), lambda bi, i: (bi, sel(i), 0, 0)),
            pl.BlockSpec((1, HG_HEAD_DIM), lambda bi, i: (0, 0)),
            pl.BlockSpec(w.shape, lambda bi, i: (0, 0)),
        ],
        out_specs=pl.BlockSpec((1, tm, d), lambda bi, i: (bi, i, 0)),
        out_shape=jax.ShapeDtypeStruct((b, s, d), F32),
        compiler_params=_cparams("parallel", "parallel"),
    )(o_f, o_b, z, sw, h, mod, ng.reshape(1, HG_HEAD_DIM), w)


def _ga_prep_kernel(z_ref, cos_ref, sin_ref, qg_ref, kg_ref, q_ref, k_ref, v_ref):
    z = z_ref[0]
    cos, sin = cos_ref[...], sin_ref[...]
    hd = GA_HEAD_DIM

    def norm_rope(x, g):
        ms = jnp.mean(x * x, axis=-1, keepdims=True)
        return _rope_half(x * lax.rsqrt(ms + EPS) * g, cos, sin, hd // 2)

    qg, kg = qg_ref[...], kg_ref[...]
    qs = [norm_rope(z[:, i * hd:(i + 1) * hd], qg) * (hd ** -0.5) for i in range(GA_Q_HEADS)]
    q_ref[0] = jnp.concatenate(qs, axis=1).astype(BF16)
    k0 = GA_Q_HEADS * hd
    ks = [norm_rope(z[:, k0 + i * hd:k0 + (i + 1) * hd], kg) for i in range(GA_KV_HEADS)]
    k_ref[0] = jnp.concatenate(ks, axis=1).astype(BF16)
    v_ref[0] = z[:, k0 + GA_KV_HEADS * hd:].astype(BF16)


def _ga_prep(z, cos, sin, qg, kg):
    b, s, n = z.shape
    tm = ROW_BLOCK
    qw, kw = GA_Q_HEADS * GA_HEAD_DIM, GA_KV_HEADS * GA_HEAD_DIM
    row = lambda width: pl.BlockSpec((1, tm, width), lambda bi, i: (bi, i, 0))
    return pl.pallas_call(
        _ga_prep_kernel,
        grid=(b, s // tm),
        in_specs=[
            row(n),
            pl.BlockSpec((tm, GA_HEAD_DIM), lambda bi, i: (i, 0)),
            pl.BlockSpec((tm, GA_HEAD_DIM), lambda bi, i: (i, 0)),
            pl.BlockSpec((1, GA_HEAD_DIM), lambda bi, i: (0, 0)),
            pl.BlockSpec((1, GA_HEAD_DIM), lambda bi, i: (0, 0)),
        ],
        out_specs=[row(qw), row(kw), row(kw)],
        out_shape=[jax.ShapeDtypeStruct((b, s, qw), BF16),
                   jax.ShapeDtypeStruct((b, s, kw), BF16),
                   jax.ShapeDtypeStruct((b, s, kw), BF16)],
        compiler_params=_cparams("parallel", "parallel"),
    )(z, cos, sin, qg.reshape(1, -1), kg.reshape(1, -1))


_GA_TQ = 256
_GA_CHUNKS = 2


def _flash_kernel(q_ref, k_ref, v_ref, o_ref, s_ref):
    hd = GA_HEAD_DIM
    group = GA_Q_HEADS // GA_KV_HEADS
    q = q_ref[0]
    tq = q.shape[0]
    qs = jnp.concatenate([q[:, i * hd:(i + 1) * hd] for i in range(group)], axis=0)
    nk = k_ref.shape[1]
    kc = nk // _GA_CHUNKS
    for c in range(_GA_CHUNKS):
        s_ref[:, c * kc:(c + 1) * kc] = lax.dot_general(qs, k_ref[0, c * kc:(c + 1) * kc, :], _NT,
                                                        preferred_element_type=F32)
    tile = lambda rows, j: s_ref[rows, j * LANES:(j + 1) * LANES]
    outs = []
    for g in range(group):
        rows = slice(g * tq, (g + 1) * tq)
        m = tile(rows, 0)
        for j in range(1, nk // LANES):
            m = jnp.maximum(m, tile(rows, j))
        m = jnp.broadcast_to(jnp.max(m, axis=-1, keepdims=True), m.shape)
        lsum = jnp.zeros_like(m)
        ps = []
        for j in range(nk // LANES):
            pj = jnp.exp(tile(rows, j) - m)
            lsum = lsum + pj
            ps.append(pj.astype(BF16))
        o = jnp.dot(jnp.concatenate(ps, axis=1), v_ref[0], preferred_element_type=F32)
        outs.append(o / jnp.sum(lsum, axis=-1, keepdims=True))
    o_ref[0] = jnp.concatenate(outs, axis=1).astype(o_ref.dtype)


def _ga_attention(q, k, v, ctx_len):
    b, s, qw = q.shape
    l = s - ctx_len
    tq = _GA_TQ
    group = GA_Q_HEADS // GA_KV_HEADS
    gw = group * GA_HEAD_DIM
    off = ctx_len // tq
    return pl.pallas_call(
        _flash_kernel,
        grid=(b, GA_KV_HEADS, l // tq),
        in_specs=[
            pl.BlockSpec((1, tq, gw), lambda bi, g, i: (bi, off + i, g)),
            pl.BlockSpec((1, s, GA_HEAD_DIM), lambda bi, g, i: (bi, 0, g)),
            pl.BlockSpec((1, s, GA_HEAD_DIM), lambda bi, g, i: (bi, 0, g)),
        ],
        out_specs=pl.BlockSpec((1, tq, gw), lambda bi, g, i: (bi, i, g)),
        out_shape=jax.ShapeDtypeStruct((b, l, qw), BF16),
        scratch_shapes=[pltpu.VMEM((group * tq, s), F32)],
        compiler_params=_cparams("parallel", "parallel", "parallel"),
    )(q, k, v)


def _out_res_kernel(a_ref, h_ref, m_ref, w_ref, o_ref):
    y = jnp.dot(a_ref[0], w_ref[...], preferred_element_type=F32)
    o_ref[0] = h_ref[0] + m_ref[0, 0][2:3] * y


def _out_res(a, h, mod, w, ctx_len):
    b, l, n = a.shape
    d = h.shape[-1]
    tm = ROW_BLOCK
    off = ctx_len // tm
    return pl.pallas_call(
        _out_res_kernel,
        grid=(b, l // tm),
        in_specs=[
            pl.BlockSpec((1, tm, n), lambda bi, i: (bi, i, 0)),
            pl.BlockSpec((1, tm, d), lambda bi, i: (bi, off + i, 0)),
            pl.BlockSpec((1, 1, 8, d), lambda bi, i: (bi, 1, 0, 0)),
            pl.BlockSpec(w.shape, lambda bi, i: (0, 0)),
        ],
        out_specs=pl.BlockSpec((1, tm, d), lambda bi, i: (bi, i, 0)),
        out_shape=jax.ShapeDtypeStruct((b, l, d), F32),
        compiler_params=_cparams("parallel", "parallel"),
    )(a, h, mod, w)


_NO_RANK = 1 << 20


def _topk_rows(s, rank=None, payload=None):
    if rank is None:
        rank = lax.broadcasted_iota(jnp.int32, s.shape, 0)
    vals, sel = [], []
    for _ in range(PEER_TOPK):
        m = jnp.max(s, axis=0, keepdims=True)
        i = jnp.min(jnp.where(s == m, rank, _NO_RANK), axis=0, keepdims=True)
        hit = rank == i
        vals.append(m)
        sel.append(i if payload is None else jnp.max(jnp.where(hit, payload, -1), axis=0, keepdims=True))
        s = jnp.where(hit, -jnp.inf, s)
    return jnp.concatenate(vals, axis=0), jnp.concatenate(sel, axis=0)


def _cand_slabs():
    k = PEER_TOPK
    slabs = [('b', 0, 0), ('b', 0, 8)] + [('b', a, 0) for a in range(1, 8)] + [('a', 8, 0)]
    rank = []
    for kind, a, b in slabs:
        for j in range(SUBLANES):
            aa, bb = (a, b + j) if kind == 'b' else (a + j, b)
            rank.append(aa * k + bb if (aa + 1) * (bb + 1) <= k else _NO_RANK)
    return slabs, np.asarray(rank, np.int32)


def _route_kernel(f_ref, wq_ref, keys_ref, rank_ref, ids_ref, gate_ref):
    f = f_ref[0]
    slabs, _ = _cand_slabs()
    rank = jnp.concatenate([rank_ref[...]] * (f.shape[0] // LANES), axis=1)
    ids, gates = [], []
    for h in range(PEER_HEADS):
        tops = []
        for p in range(2):
            c0 = (h * 2 + p) * PEER_HALF
            qhp = jnp.dot(f, wq_ref[:, c0:c0 + PEER_HALF], preferred_element_type=F32)
            st = lax.dot_general(keys_ref[h, p], qhp, _NT,
                                 precision=HIGHEST, preferred_element_type=F32)
            tops.append(_topk_rows(st))
        (s1, i1), (s2, i2) = tops
        pick = lambda x, y, kind, a, b: ((x[a:a + 1], y[b:b + SUBLANES]) if kind == 'b'
                                         else (x[a:a + SUBLANES], y[b:b + 1]))
        cand = jnp.concatenate([sum(pick(s1, s2, *sl)) for sl in slabs], axis=0)
        cand = jnp.where(rank < _NO_RANK, cand, -jnp.inf)
        eid = jnp.concatenate([(lambda x, y: x * PEER_N_KEYS + y)(*pick(i1, i2, *sl)) for sl in slabs],
                              axis=0)
        sc, e = _topk_rows(cand, rank, eid)
        ex = jnp.exp(sc - jnp.max(sc, axis=0, keepdims=True))
        gates.append(ex / jnp.sum(ex, axis=0, keepdims=True))
        ids.append(e)
    ids_ref[0] = jnp.concatenate(ids, axis=0).astype(F32).T.astype(jnp.int32)
    gate_ref[0] = jnp.concatenate(gates, axis=0).T


def _norm_mod_kernel(h_ref, m_ref, g_ref, f_ref, *, row0):
    x = h_ref[0]
    ms = jnp.mean(x * x, axis=-1, keepdims=True)
    y = x * lax.rsqrt(ms + EPS) * g_ref[...]
    m = m_ref[0, 0]
    f_ref[0] = (y * (1.0 + m[row0 + 1:row0 + 2]) + m[row0:row0 + 1]).astype(BF16)


def _norm_mod(h, mod, g, *, row0, ctx_blocks):
    b, r, d = h.shape
    tm = ROW_BLOCK
    sel = (lambda i: jnp.minimum(i // ctx_blocks, 1)) if ctx_blocks else (lambda i: 1)
    return pl.pallas_call(
        functools.partial(_norm_mod_kernel, row0=row0),
        grid=(b, r // tm),
        in_specs=[
            pl.BlockSpec((1, tm, d), lambda bi, i: (bi, i, 0)),
            pl.BlockSpec((1, 1, 8, d), lambda bi, i: (bi, sel(i), 0, 0)),
            pl.BlockSpec((1, d), lambda bi, i: (0, 0)),
        ],
        out_specs=pl.BlockSpec((1, tm, d), lambda bi, i: (bi, i, 0)),
        out_shape=jax.ShapeDtypeStruct((b, r, d), BF16),
        compiler_params=_cparams("parallel", "parallel"),
    )(h, mod, g.reshape(1, d))


def _route(f, wq, keys):
    b, r, n = f.shape
    tm = ROW_BLOCK
    rank = np.tile(_cand_slabs()[1][:, None], (1, LANES))
    return pl.pallas_call(
        _route_kernel,
        grid=(b, r // tm),
        in_specs=[
            pl.BlockSpec((1, tm, n), lambda bi, i: (bi, i, 0)),
            pl.BlockSpec(wq.shape, lambda bi, i: (0, 0)),
            pl.BlockSpec(keys.shape, lambda bi, i: (0, 0, 0, 0)),
            pl.BlockSpec(rank.shape, lambda bi, i: (0, 0)),
        ],
        out_specs=[pl.BlockSpec((1, tm, PEER_PAIRS), lambda bi, i: (bi, i, 0)),
                   pl.BlockSpec((1, tm, PEER_PAIRS), lambda bi, i: (bi, i, 0))],
        out_shape=[jax.ShapeDtypeStruct((b, r, PEER_PAIRS), jnp.int32),
                   jax.ShapeDtypeStruct((b, r, PEER_PAIRS), F32)],
        compiler_params=_cparams("parallel", "parallel"),
    )(f, wq, keys, jnp.asarray(rank))


_EX_TOK = 8
_EX_HALVES = 4
_EX_AHEAD = 2
_EX_NP = _EX_TOK * PEER_PAIRS
_ROW_TILES = 8
_ROW_PITCH = 12
_EX_GROUPS = 2 * _ROW_TILES


def _expert_kernel(ids_ref, idsn_ref, f_ref, gate_ref, h_ref, m_ref, fg_ref, ex_ref, uv_ref, o_ref,
                   *scratch, n_steps, final):
    bufs, sem = scratch[:_EX_HALVES], scratch[_EX_HALVES]
    step = pl.program_id(0) * pl.num_programs(1) + pl.program_id(1)
    per_group = _EX_NP // _EX_GROUPS

    def start_copy(e, dst, p, sem_i, priority):
        pltpu.make_async_copy(uv_ref.at[e], dst.at[pl.ds(p * _ROW_PITCH, _ROW_TILES), :],
                              sem.at[sem_i]).start(priority=priority)

    def wait_all(dst, sem_i):
        rows = pl.ds(0, _EX_NP * _ROW_TILES)
        pltpu.make_async_copy(dst.at[rows, :], dst.at[rows, :], sem.at[sem_i]).wait()

    def half(cur, half_i, issue_ids, issue_off, other, other_sem):
        rows = slice(half_i * _EX_TOK, (half_i + 1) * _EX_TOK)
        f = f_ref[0, rows, :]

        def issue(g):
            for u in range(per_group):
                p = g * per_group + u
                start_copy(issue_ids[0, issue_off + p], other, p, other_sem, u % 2)

        words = lambda r: pltpu.bitcast(cur[pl.ds(r, _EX_NP, stride=_ROW_PITCH), :], BF16)
        a = jnp.zeros((_EX_TOK, _EX_NP), F32)
        for r in range(_ROW_TILES):
            issue(r)
            u = pltpu.bitcast(lax.shift_left(cur[pl.ds(r, _EX_NP, stride=_ROW_PITCH), :], 16), F32)
            a = a + lax.dot_general(f[:, r * LANES:(r + 1) * LANES], u.astype(BF16), _NT,
                                    preferred_element_type=F32)
        act = 0.5 * a * (1.0 + lax.erf(a * (2.0 ** -0.5)))
        gate = jnp.concatenate([gate_ref[0, rows, :]] * _EX_TOK, axis=1)
        own = (lax.broadcasted_iota(jnp.int32, a.shape, 1) // PEER_PAIRS
               == lax.broadcasted_iota(jnp.int32, a.shape, 0))
        ga = jnp.where(own, gate * act, 0.0).astype(BF16)
        coef = jnp.concatenate(
            [jnp.dot(ga[:, t * PEER_PAIRS:(t + 1) * PEER_PAIRS], ex_ref[...], preferred_element_type=F32)
             for t in range(_EX_TOK)], axis=1).astype(BF16)
        ys = []
        for r in range(_ROW_TILES):
            issue(_ROW_TILES + r)
            ys.append(jnp.dot(coef, words(r), preferred_element_type=F32))
        hn = h_ref[0, rows, :] + m_ref[0, 0][5:6] * jnp.concatenate(ys, axis=1)
        if final:
            ms = jnp.mean(hn * hn, axis=-1, keepdims=True)
            hn = hn * lax.rsqrt(ms + EPS) * fg_ref[...]
        o_ref[0, rows, :] = hn

    @pl.when(step == 0)
    def _():
        for k in range(_EX_AHEAD):
            def body(p, _, k=k):
                start_copy(ids_ref[0, k * _EX_NP + p], bufs[k], p, k, 0)
                return 0
            lax.fori_loop(0, _EX_NP, body, 0)

    for k in range(_EX_HALVES):
        wait_all(bufs[k], k)
        tgt = k + _EX_AHEAD
        if tgt < _EX_HALVES:
            half(bufs[k], k, ids_ref, tgt * _EX_NP, bufs[tgt], tgt)
        else:
            half(bufs[k], k, idsn_ref, (tgt - _EX_HALVES) * _EX_NP, bufs[tgt - _EX_HALVES],
                 tgt - _EX_HALVES)

    @pl.when(step == n_steps - 1)
    def _():
        for k in range(_EX_AHEAD):
            wait_all(bufs[k], k)


def _experts(ids, gates, f, h, mod, uv, final_g, *, ctx_blocks, final):
    b, r, d = h.shape
    tok = _EX_HALVES * _EX_TOK
    nb = r // tok
    n_steps = b * nb
    ids_flat = ids.reshape(n_steps, 1, tok * PEER_PAIRS)
    per_ctx = ctx_blocks * (ROW_BLOCK // tok)
    sel = (lambda i: jnp.minimum(i // per_ctx, 1)) if ctx_blocks else (lambda i: 1)
    blk = lambda width: pl.BlockSpec((1, tok, width), lambda bi, i: (bi, i, 0))
    lin = lambda bi, i: bi * nb + i
    expand = np.zeros((PEER_PAIRS, 2 * PEER_PAIRS), np.float32)
    expand[np.arange(PEER_PAIRS), 2 * np.arange(PEER_PAIRS) + 1] = 1.0
    buf = pltpu.VMEM((_EX_NP * _ROW_PITCH, LANES), jnp.int32)
    return pl.pallas_call(
        functools.partial(_expert_kernel, n_steps=n_steps, final=final),
        grid=(b, nb),
        in_specs=[
            pl.BlockSpec((None, 1, tok * PEER_PAIRS), lambda bi, i: (lin(bi, i), 0, 0),
                         memory_space=pltpu.SMEM),
            pl.BlockSpec((None, 1, tok * PEER_PAIRS),
                         lambda bi, i: (jnp.minimum(lin(bi, i) + 1, n_steps - 1), 0, 0),
                         memory_space=pltpu.SMEM),
            blk(d), blk(PEER_PAIRS), blk(d),
            pl.BlockSpec((1, 1, 8, d), lambda bi, i: (bi, sel(i), 0, 0)),
            pl.BlockSpec((1, d), lambda bi, i: (0, 0)),
            pl.BlockSpec(expand.shape, lambda bi, i: (0, 0)),
            pl.BlockSpec(memory_space=pl.ANY),
        ],
        out_specs=blk(d),
        out_shape=jax.ShapeDtypeStruct((b, r, d), F32),
        scratch_shapes=[buf] * _EX_HALVES + [pltpu.SemaphoreType.DMA((_EX_HALVES,))],
        compiler_params=_cparams("arbitrary", "arbitrary"),
    )(ids_flat, ids_flat, f, gates, h, mod, final_g.reshape(1, d), jnp.asarray(expand, BF16), uv)


def _pack_experts(u, v):
    ub = lax.bitcast_convert_type(u.astype(BF16), jnp.uint16).astype(jnp.uint32)
    vb = lax.bitcast_convert_type(v.astype(BF16), jnp.uint16).astype(jnp.uint32)
    words = lax.bitcast_convert_type(ub | (vb << 16), jnp.int32)
    return words.reshape(u.shape[0], _ROW_TILES, LANES)


def _rope_tables(seq, ctx_len, head_dim):
    n_rows = seq // GRID_W
    row = jnp.repeat(jnp.arange(n_rows), GRID_W).astype(F32)
    col = jnp.tile(jnp.arange(GRID_W), n_rows).astype(F32)
    n_freq = head_dim // 4
    inv = ROPE_THETA ** (-jnp.arange(n_freq, dtype=F32) / n_freq)
    ang = jnp.concatenate([row[:, None] * inv, col[:, None] * inv], axis=-1)
    cos = jnp.concatenate([jnp.cos(ang), jnp.cos(ang)], axis=-1)
    sin = jnp.concatenate([-jnp.sin(ang), jnp.sin(ang)], axis=-1)
    reps = LANES // head_dim
    cos, sin = jnp.tile(cos, (1, reps)), jnp.tile(sin, (1, reps))
    cos = jnp.concatenate([jnp.ones((ctx_len, LANES), F32), cos], axis=0)
    sin = jnp.concatenate([jnp.zeros((ctx_len, LANES), F32), sin], axis=0)
    return cos, sin


def kernel(x, c, ctx, c_ctx, ada_w, ada_b, norm_mix_g, norm_ffn_g, ab_w_in, ab_w_out, hg_lb_logits,
           hg_norm_g, sw_sink, ga_w_in, ga_w_out, ga_q_norm_g, ga_k_norm_g, peer_w_q, peer_sub_keys,
           peer_u, peer_v, final_norm_g):
    b, l, d = x.shape
    lc = ctx.shape[1]
    assert lc % ROW_BLOCK == 0 and l % ROW_BLOCK == 0 and ada_w.shape[0] == 2
    ctx_blocks = lc // ROW_BLOCK
    h = jnp.concatenate([ctx, x], axis=1)

    n_mod = -(-(b + 1) // SUBLANES) * SUBLANES
    cc = jnp.zeros((n_mod, d), F32).at[:b].set(c).at[b].set(c_ctx)
    mod_all = _ada(cc, ada_w, ada_b)

    def mod_for(layer):
        lat = mod_all[layer, :b].reshape(b, 1, 6, d)
        cx = jnp.broadcast_to(mod_all[layer, b].reshape(1, 1, 6, d), (b, 1, 6, d))
        m = jnp.concatenate([cx, lat], axis=1)
        return jnp.pad(m, ((0, 0), (0, 0), (0, 2), (0, 0)))

    lbs = jnp.cumsum(jax.nn.softmax(hg_lb_logits.astype(F32), axis=0), axis=0)

    m0 = mod_for(0)
    z = _nmm(h, m0, norm_mix_g[0], ab_w_in[0].astype(BF16), row0=0, ctx_blocks=ctx_blocks)
    o_f, o_b = _hgrn(z, lbs[0], lc)
    cos_sw, sin_sw = _rope_tables(l, lc, SW_HEAD_DIM)
    sw = _sw_attention(z, cos_sw, sin_sw, sw_sink[0], lc)
    h = _ab_out(o_f, o_b, z, sw, h, m0, hg_norm_g[0], ab_w_out[0].astype(BF16), ctx_blocks)
    f = _norm_mod(h, m0, norm_ffn_g[0], row0=3, ctx_blocks=ctx_blocks)
    ids, gates = _route(f, peer_w_q[0].astype(BF16), peer_sub_keys[0])
    h = _experts(ids, gates, f, h, m0, _pack_experts(peer_u[0], peer_v[0]), final_norm_g,
                 ctx_blocks=ctx_blocks, final=False)

    m1 = mod_for(1)
    z = _nmm(h, m1, norm_mix_g[1], ga_w_in[0].astype(BF16), row0=0, ctx_blocks=ctx_blocks)
    cos_ga, sin_ga = _rope_tables(l, lc, GA_HEAD_DIM)
    qg, kg, vg = _ga_prep(z, cos_ga, sin_ga, ga_q_norm_g[0], ga_k_norm_g[0])
    att = _ga_attention(qg, kg, vg, lc)
    hl = _out_res(att, h, m1, ga_w_out[0].astype(BF16), lc)
    f = _norm_mod(hl, m1, norm_ffn_g[1], row0=3, ctx_blocks=0)
    ids, gates = _route(f, peer_w_q[1].astype(BF16), peer_sub_keys[1])
    return _experts(ids, gates, f, hl, m1, _pack_experts(peer_u[1], peer_v[1]), final_norm_g,
                    ctx_blocks=0, final=True)
```
